```python
import math
import jax, jax.numpy as jnp
from jax import lax
import numpy as np

D_MODEL = 1024
BATCH = 8
SEQ = 4096
DEPTH = 1

ATT_PATTERNS = ((128, 1), (512, 4), (2048, 16))
ATT_GROUPS = len(ATT_PATTERNS)
ATT_HEADS = 8
ATT_HEAD_DIM = 64
ATT_WIDTH = ATT_HEADS * ATT_HEAD_DIM
ATT_BLOCK = 128
RWKV_WIDTH = D_MODEL
RWKV_HEAD_DIM = 64
RWKV_HEADS = RWKV_WIDTH // RWKV_HEAD_DIM
DECAY_LORA = 64
AAA_LORA = 64
GATE_LORA = 160
D_FF = 2816
CONV_WIDTH = 3
N_BRANCHES = 2
ATT_IN = ATT_GROUPS * 3 * ATT_WIDTH
RWKV_IN = 3 * RWKV_WIDTH + DECAY_LORA + AAA_LORA + GATE_LORA
GATE_IN = N_BRANCHES * D_MODEL
N_IN = ATT_IN + RWKV_IN + GATE_IN
RMS_EPS = 1e-6
GN_EPS = 64e-5

kernel_name = 'hybrid_dilated_attn_rwkv7_convffn_adaln'


def rms_norm(x, w):
    xf = x.astype(jnp.float32)
    y = xf * lax.rsqrt(jnp.mean(xf * xf, axis=-1, keepdims=True) + RMS_EPS)
    return (y * w).astype(x.dtype)


def dilated_window_attention(q, k, v, window, dilation):
    b, s, h, e = q.shape
    back = window // dilation
    sub_len = -(-s // dilation)
    n_blk = -(-sub_len // ATT_BLOCK)
    s_pad = n_blk * ATT_BLOCK * dilation

    def to_blocks(t):
        t = jnp.pad(t, ((0, 0), (0, s_pad - s), (0, 0), (0, 0)))
        return t.reshape(b, n_blk, ATT_BLOCK, dilation, h, e)

    def with_prev(t):
        prev = jnp.concatenate([jnp.zeros_like(t[:, :1]), t[:, :-1]], axis=1)
        return jnp.concatenate([prev, t], axis=2)

    qb = to_blocks(q)
    kc = with_prev(to_blocks(k))
    vc = with_prev(to_blocks(v))
    scores = jnp.einsum('bnqrhe,bnkrhe->bnrhqk', qb, kc).astype(jnp.float32) * (e ** -0.5)
    qi = jnp.arange(ATT_BLOCK)[:, None]
    kj = jnp.arange(2 * ATT_BLOCK)[None, :]
    dist = qi + ATT_BLOCK - kj
    kpos = jnp.arange(n_blk)[:, None] * ATT_BLOCK + kj - ATT_BLOCK
    valid = ((dist >= 0) & (dist <= back))[None] & (kpos >= 0)[:, None, :]
    scores = jnp.where(valid[None, :, None, None], scores, -jnp.inf)
    m = jnp.max(scores, axis=-1, keepdims=True)
    p = jnp.exp(scores - m)
    den = jnp.sum(p, axis=-1)
    num = jnp.einsum('bnrhqk,bnkrhe->bnqrhe', p, vc.astype(jnp.float32))
    den_q = jnp.transpose(den, (0, 1, 4, 2, 3))
    out = (num / den_q[..., None]).reshape(b, s_pad, h, e)[:, :s]
    lse = (jnp.transpose(m[..., 0], (0, 1, 4, 2, 3)) + jnp.log(den_q)).reshape(b, s_pad, h)[:, :s]
    return out, lse


def attention_mixer(z):
    b, s, _ = z.shape
    qkv = z.reshape(b, s, ATT_GROUPS, 3, ATT_HEADS, ATT_HEAD_DIM)
    outs, lses = [], []
    for g, (window, dilation) in enumerate(ATT_PATTERNS):
        o, l = dilated_window_attention(qkv[:, :, g, 0], qkv[:, :, g, 1], qkv[:, :, g, 2], window, dilation)
        outs.append(o)
        lses.append(l)
    wts = jax.nn.softmax(jnp.stack(lses), axis=0)
    out = jnp.sum(wts[..., None] * jnp.stack(outs), axis=0)
    return out.reshape(b, s, ATT_WIDTH).astype(z.dtype)


def rwkv7_mixer(z, mu, w0, w2, a0, a2, g2, k_k, k_a, r_k, lnx_w, lnx_b):
    f32 = jnp.float32
    b, s, _ = z.shape
    z_prev = jnp.pad(z, ((0, 0), (1, 0), (0, 0)))[:, :s]
    z = z + (z_prev - z) * mu
    c = RWKV_WIDTH
    r, k, v, w_low, a_low, g_low = jnp.split(
        z, [c, 2 * c, 3 * c, 3 * c + DECAY_LORA, 3 * c + DECAY_LORA + AAA_LORA], axis=-1)
    w_log = -jax.nn.softplus(-(w0 + jnp.tanh(w_low) @ w2).astype(f32)) - 0.5
    decay = jnp.exp(-jnp.exp(w_log))
    a = jax.nn.sigmoid((a0 + a_low @ a2).astype(f32))
    g = jax.nn.sigmoid(g_low) @ g2
    k_mod = k.astype(f32) * (1.0 + (a - 1.0) * k_a)

    def heads(t):
        return t.astype(f32).reshape(b, s, RWKV_HEADS, RWKV_HEAD_DIM)

    kk = heads(k * k_k)
    kk = kk / jnp.maximum(jnp.sqrt(jnp.sum(kk * kk, axis=-1, keepdims=True)), 1e-12)
    r_h, k_h, v_h, w_h, a_h = heads(r), heads(k_mod), heads(v), heads(decay), heads(a)

    def step(state, inp):
        r_t, w_t, k_t, v_t, aa_t, bb_t = inp
        sa = jnp.einsum('bhvk,bhk->bhv', state, aa_t)
        state = state * w_t[:, :, None, :] + sa[..., None] * bb_t[:, :, None, :] + v_t[..., None] * k_t[:, :, None, :]
        return state, jnp.einsum('bhvk,bhk->bhv', state, r_t)

    tm = lambda t: jnp.swapaxes(t, 0, 1)
    state0 = jnp.zeros((b, RWKV_HEADS, RWKV_HEAD_DIM, RWKV_HEAD_DIM), f32)
    _, y = lax.scan(step, state0, (tm(r_h), tm(w_h), tm(k_h), tm(v_h), tm(-kk), tm(kk * a_h)))
    y = tm(y)
    mean = jnp.mean(y, axis=-1, keepdims=True)
    var = jnp.mean(jnp.square(y - mean), axis=-1, keepdims=True)
    y = ((y - mean) * lax.rsqrt(var + GN_EPS)).reshape(b, s, c) * lnx_w + lnx_b
    bonus = (jnp.sum(r_h * k_h * r_k, axis=-1, keepdims=True) * v_h).reshape(b, s, c)
    return ((y + bonus) * g).astype(z.dtype)


def conv_ffn(h, w_up, conv_w, conv_b, w_down):
    s = h.shape[1]
    u = h @ w_up
    up = jnp.pad(u, ((0, 0), (CONV_WIDTH - 1, 0), (0, 0)))
    u = conv_b + sum(conv_w[j] * up[:, j:j + s] for j in range(CONV_WIDTH))
    gate, val = jnp.split(u, 2, axis=-1)
    return (jax.nn.silu(gate) * val) @ w_down


def setup_inputs(seed: int = 0) -> dict:
    key = jax.random.key(seed)
    ks = iter(jax.random.split(key, 32))
    f32 = jnp.float32
    L, D, C = DEPTH, D_MODEL, RWKV_WIDTH

    def nrm(shape, scale):
        return jax.random.normal(next(ks), shape, f32) * scale

    ramp = (jnp.arange(C, dtype=f32) / (C - 1)) ** 0.85
    inputs = {}
    inputs['x'] = nrm((BATCH, SEQ, D), 1.0)
    inputs['c'] = nrm((BATCH, D), 1.0)
    inputs['w_ada'] = nrm((L, D, 6 * D), 0.3 * D ** -0.5)
    inputs['b_ada'] = nrm((L, 6 * D), 0.02)
    inputs['norm1_w'] = 1.0 + nrm((L, D), 0.05)
    inputs['w_in'] = nrm((L, D, N_IN), D ** -0.5)
    inputs['b_gate'] = nrm((L, GATE_IN), 0.1)
    inputs['mu_shift'] = jax.random.uniform(next(ks), (L, RWKV_IN), f32)
    inputs['w0'] = -6.5 + 5.0 * ramp + nrm((L, C), 0.1)
    inputs['w2'] = nrm((L, DECAY_LORA, C), 0.1 * DECAY_LORA ** -0.5)
    inputs['a0'] = nrm((L, C), 0.1)
    inputs['a2'] = nrm((L, AAA_LORA, C), AAA_LORA ** -0.5)
    inputs['g2'] = nrm((L, GATE_LORA, C), GATE_LORA ** -0.5)
    inputs['k_k'] = 0.85 + nrm((L, C), 0.05)
    inputs['k_a'] = 1.0 + nrm((L, C), 0.05)
    inputs['r_k'] = nrm((L, RWKV_HEADS, RWKV_HEAD_DIM), 0.1)
    inputs['lnx_w'] = 1.0 + nrm((L, C), 0.05)
    inputs['lnx_b'] = nrm((L, C), 0.02)
    inputs['w_att_out'] = nrm((L, ATT_WIDTH, D), ATT_WIDTH ** -0.5)
    inputs['w_rwkv_out'] = nrm((L, C, D), C ** -0.5)
    inputs['w_o'] = nrm((L, D, D), D ** -0.5)
    inputs['norm2_w'] = 1.0 + nrm((L, D), 0.05)
    inputs['w_up'] = nrm((L, D, 2 * D_FF), D ** -0.5)
    inputs['conv_w'] = nrm((L, CONV_WIDTH, 2 * D_FF), CONV_WIDTH ** -0.5)
    inputs['conv_b'] = nrm((L, 2 * D_FF), 0.02)
    inputs['w_down'] = nrm((L, D_FF, D), D_FF ** -0.5)
    inputs['norm_f_w'] = 1.0 + nrm((D,), 0.05)
    return inputs


def reference(x, c, w_ada, b_ada, norm1_w, w_in, b_gate, mu_shift, w0, w2, a0, a2, g2, k_k, k_a, r_k,
              lnx_w, lnx_b, w_att_out, w_rwkv_out, w_o, norm2_w, w_up, conv_w, conv_b, w_down, norm_f_w):
    for l in range(DEPTH):
        ada = (c @ w_ada[l] + b_ada[l])[:, None, :]
        sh1, sc1, gt1, sh2, sc2, gt2 = jnp.split(ada, 6, axis=-1)
        h = rms_norm(x, norm1_w[l]) * (1.0 + sc1) + sh1
        proj = h @ w_in[l]
        att_in, rwkv_in, gate_in = jnp.split(proj, [ATT_IN, ATT_IN + RWKV_IN], axis=-1)
        y_att = attention_mixer(att_in) @ w_att_out[l]
        y_rwkv = rwkv7_mixer(rwkv_in, mu_shift[l], w0[l], w2[l], a0[l], a2[l], g2[l], k_k[l], k_a[l],
                             r_k[l], lnx_w[l], lnx_b[l]) @ w_rwkv_out[l]
        g_att, g_rwkv = jnp.split(jax.nn.sigmoid(gate_in + b_gate[l]), N_BRANCHES, axis=-1)
        x = x + gt1 * ((g_att * y_att + g_rwkv * y_rwkv) @ w_o[l])
        h = rms_norm(x, norm2_w[l]) * (1.0 + sc2) + sh2
        x = x + gt2 * conv_ffn(h, w_up[l], conv_w[l], conv_b[l], w_down[l])
    return rms_norm(x, norm_f_w)
```

```python
import functools

import jax
import jax.numpy as jnp
from jax import lax
from jax.experimental import pallas as pl
from jax.experimental.pallas import tpu as pltpu

F32 = jnp.float32
BF16 = jnp.bfloat16

ATT_PATTERNS = ((128, 1), (512, 4), (2048, 16))
ATT_HEADS = 8
HEAD_DIM = 64
ATT_WIDTH = ATT_HEADS * HEAD_DIM
ATT_BLOCK = 128
RWKV_HEADS = 16
DECAY_LORA = 64
AAA_LORA = 64
GATE_LORA = 160
LORA_PAD = 384
CONV_WIDTH = 3
RMS_EPS = 1e-6
GN_EPS = 64e-5
CHUNK = 64
VMEM_LIMIT = 48 * 1024 * 1024


def _dot(a, b):
    return jnp.dot(a, b, preferred_element_type=F32)


def _split3(x):
    hi = x.astype(BF16)
    r1 = x - hi.astype(F32)
    mid = r1.astype(BF16)
    lo = (r1 - mid.astype(F32)).astype(BF16)
    return hi, mid, lo


def _params(sem):
    return pltpu.CompilerParams(dimension_semantics=sem, vmem_limit_bytes=VMEM_LIMIT)


def _ada_kernel(c_ref, w_ref, b_ref, o_ref):
    c = c_ref[...]
    w = w_ref[...]
    ch, cl, _ = _split3(c)
    wh, wl, _ = _split3(w)
    o_ref[...] = _dot(ch, wh) + _dot(cl, wh) + _dot(ch, wl) + b_ref[...]


def _ada(c, w_ada, b_ada):
    b, d = c.shape
    n = w_ada.shape[1]
    tn = 1536
    return pl.pallas_call(
        _ada_kernel,
        grid=(n // tn,),
        in_specs=[pl.BlockSpec((b, d), lambda j: (0, 0)),
                  pl.BlockSpec((d, tn), lambda j: (0, j)),
                  pl.BlockSpec((1, tn), lambda j: (0, j))],
        out_specs=pl.BlockSpec((b, tn), lambda j: (0, j)),
        out_shape=jax.ShapeDtypeStruct((b, n), F32),
        compiler_params=_params(("arbitrary",)),
    )(c, w_ada, b_ada.reshape(1, n))


def _adaln(x, nw, sc, sh):
    y = x * lax.rsqrt(jnp.mean(x * x, axis=-1, keepdims=True) + RMS_EPS)
    return (y * nw) * (1.0 + sc) + sh


def _norm_mm_kernel(x_ref, nw_ref, sc_ref, sh_ref, w_ref, o_ref, h_ref):
    @pl.when(pl.program_id(1) == 0)
    def _():
        h_ref[...] = _adaln(x_ref[...], nw_ref[...], sc_ref[0], sh_ref[0]).astype(BF16)

    o_ref[...] = _dot(h_ref[...], w_ref[...]).astype(o_ref.dtype)


def _norm_mm(x2, nw, sc, sh, w, out_dtype, tm, tn, seq):
    m, d = x2.shape
    n = w.shape[1]
    per_b = seq // tm
    return pl.pallas_call(
        _norm_mm_kernel,
        grid=(m // tm, n // tn),
        in_specs=[pl.BlockSpec((tm, d), lambda i, j: (i, 0)),
                  pl.BlockSpec((1, d), lambda i, j: (0, 0)),
                  pl.BlockSpec((1, 1, d), lambda i, j: (i // per_b, 0, 0)),
                  pl.BlockSpec((1, 1, d), lambda i, j: (i // per_b, 0, 0)),
                  pl.BlockSpec((d, tn), lambda i, j: (0, j))],
        out_specs=pl.BlockSpec((tm, tn), lambda i, j: (i, j)),
        out_shape=jax.ShapeDtypeStruct((m, n), out_dtype),
        scratch_shapes=[pltpu.VMEM((tm, d), BF16)],
        compiler_params=_params(("parallel", "arbitrary")),
    )(x2, nw, sc, sh, w)


def _att_kernel(q_ref, kp_ref, kc_ref, vp_ref, vc_ref, o_ref, l_ref):
    n = pl.program_id(2)
    qi = lax.broadcasted_iota(jnp.int32, (ATT_BLOCK, 2 * ATT_BLOCK), 0)
    kj = lax.broadcasted_iota(jnp.int32, (ATT_BLOCK, 2 * ATT_BLOCK), 1)
    first_key = jnp.where(n == 0, ATT_BLOCK, 0)
    valid = (kj >= qi) & (kj <= qi + ATT_BLOCK) & (kj >= first_key)
    q = q_ref[0]
    k = jnp.concatenate([kp_ref[0], kc_ref[0]], axis=0)
    v = jnp.concatenate([vp_ref[0], vc_ref[0]], axis=0)
    outs, lses = [], []
    for h in range(ATT_HEADS):
        sl = slice(h * HEAD_DIM, (h + 1) * HEAD_DIM)
        s = lax.dot_general(q[:, sl], k[:, sl], (((1,), (1,)), ((), ())),
                            preferred_element_type=F32) * (HEAD_DIM ** -0.5)
        s = jnp.where(valid, s, -jnp.inf)
        m = jnp.max(s, axis=-1, keepdims=True)
        p = jnp.exp(s - m)
        den = jnp.sum(p, axis=-1, keepdims=True)
        num = _dot(p.astype(BF16), v[:, sl])
        outs.append(num / den)
        lses.append(jnp.broadcast_to(m + jnp.log(den), (ATT_BLOCK, HEAD_DIM)))
    o_ref[0] = jnp.concatenate(outs, axis=-1)
    l_ref[0] = jnp.concatenate(lses, axis=-1)


def _attention_group(proj_att, g, dilation, batch, seq):
    n_in = proj_att.shape[-1]
    sub = seq // dilation
    n_blk = sub // ATT_BLOCK
    pv = proj_att.reshape(batch, sub, dilation * n_in)
    cpb = n_in // ATT_WIDTH
    base = g * 3

    def spec(which, prev):
        if prev:
            return pl.BlockSpec((1, ATT_BLOCK, ATT_WIDTH),
                                lambda b, r, n: (b, jnp.maximum(n - 1, 0), r * cpb + base + which))
        return pl.BlockSpec((1, ATT_BLOCK, ATT_WIDTH), lambda b, r, n: (b, n, r * cpb + base + which))

    out_spec = pl.BlockSpec((1, ATT_BLOCK, ATT_WIDTH), lambda b, r, n: (b, n, r))
    o, l = pl.pallas_call(
        _att_kernel,
        grid=(batch, dilation, n_blk),
        in_specs=[spec(0, False), spec(1, True), spec(1, False), spec(2, True), spec(2, False)],
        out_specs=[out_spec, out_spec],
        out_shape=[jax.ShapeDtypeStruct((batch, sub, dilation * ATT_WIDTH), F32)] * 2,
        compiler_params=_params(("parallel", "parallel", "arbitrary")),
    )(pv, pv, pv, pv, pv)
    return o.reshape(batch, seq, ATT_WIDTH), l.reshape(batch, seq, ATT_WIDTH)


def _rwkv_prep_kernel(z_ref, lo_ref, mu_ref, mulo_ref, w0_ref, w2_ref, a0_ref, a2_ref, g2_ref, kk_ref, ka_ref,
                      r_out, lw_out, km_out, v_out, kn_out, a_out, g_out, carry_ref, carry_lo_ref):
    t = pl.program_id(1)

    @pl.when(t == 0)
    def _():
        carry_ref[...] = jnp.zeros_like(carry_ref)
        carry_lo_ref[...] = jnp.zeros_like(carry_lo_ref)

    def shift(z, carry, mu):
        rows = lax.broadcasted_iota(jnp.int32, z.shape, 0)
        prev = jnp.where(rows == 0, carry, pltpu.roll(z, 1, axis=0))
        return z + (prev - z) * mu

    z = z_ref[...]
    zl = lo_ref[...]
    tm = z.shape[0]
    c = z.shape[1] // 3
    zs = shift(z, carry_ref[...], mu_ref[...])
    zls = shift(zl, carry_lo_ref[...], mulo_ref[...])
    carry_ref[...] = z[tm - 1:tm, :]
    carry_lo_ref[...] = zl[tm - 1:tm, :]

    r = zs[:, :c]
    k = zs[:, c:2 * c]
    v = zs[:, 2 * c:]
    w_low = zls[:, :DECAY_LORA]
    a_low = zls[:, DECAY_LORA:DECAY_LORA + AAA_LORA]
    g_low = zls[:, DECAY_LORA + AAA_LORA:DECAY_LORA + AAA_LORA + GATE_LORA]

    wpre = w0_ref[...] + _dot(jnp.tanh(w_low).astype(BF16), w2_ref[...])
    w_log = -jax.nn.softplus(-wpre) - 0.5
    lw_out[...] = -jnp.exp(w_log)
    a = jax.nn.sigmoid(a0_ref[...] + _dot(a_low.astype(BF16), a2_ref[...]))
    g_out[...] = _dot(jax.nn.sigmoid(g_low).astype(BF16), g2_ref[...])
    r_out[...] = r
    v_out[...] = v
    a_out[...] = a
    km_out[...] = k * (1.0 + (a - 1.0) * ka_ref[...])
    kn_out[...] = k * kk_ref[...]


def _rwkv_prep(z_main, z_lora, mu, mu_lo, w0, w2, a0, a2, g2, k_k, k_a, batch, seq, tm):
    m, c3 = z_main.shape
    c = c3 // 3
    per_b = seq // tm
    row = lambda b, t: (b * per_b + t, 0)
    fixed = lambda b, t: (0, 0)
    vec = pl.BlockSpec((1, c), fixed)
    outs = [jax.ShapeDtypeStruct((m, c), F32)] * 7
    return pl.pallas_call(
        _rwkv_prep_kernel,
        grid=(batch, per_b),
        in_specs=[pl.BlockSpec((tm, c3), row), pl.BlockSpec((tm, LORA_PAD), row),
                  pl.BlockSpec((1, c3), fixed), pl.BlockSpec((1, LORA_PAD), fixed),
                  vec, pl.BlockSpec((DECAY_LORA, c), fixed),
                  vec, pl.BlockSpec((AAA_LORA, c), fixed),
                  pl.BlockSpec((GATE_LORA, c), fixed), vec, vec],
        out_specs=[pl.BlockSpec((tm, c), row)] * 7,
        out_shape=outs,
        scratch_shapes=[pltpu.VMEM((1, c3), F32), pltpu.VMEM((1, LORA_PAD), F32)],
        compiler_params=_params(("parallel", "arbitrary")),
    )(z_main, z_lora, mu, mu_lo, w0, w2, a0, a2, g2, k_k, k_a)


def _bmm(a, b):
    return lax.dot_general(a, b, (((2,), (1,)), ((0,), (0,))), preferred_element_type=F32)


def _bmm_nt(a, b):
    return lax.dot_general(a, b, (((2,), (2,)), ((0,), (0,))), preferred_element_type=F32)


def _rwkv_chunk_kernel(r_ref, lw_ref, km_ref, v_ref, kn_ref, a_ref, g_ref, rk_ref, lnw_ref, lnb_ref, tri_ref,
                       o_ref, state_ref, x_ref, z_ref, vh_ref, rkm_ref):
    C = CHUNK
    H = RWKV_HEADS
    N = HEAD_DIM

    @pl.when(pl.program_id(1) == 0)
    def _():
        state_ref[...] = jnp.zeros_like(state_ref)

    r = r_ref[...]
    lw = lw_ref[...]
    km = km_ref[...]
    v = v_ref[...]
    a = a_ref[...]

    tri = tri_ref[...]
    l1, l2, l3 = _split3(lw)
    cum = _dot(tri, l1) + _dot(tri, l2) + _dot(tri, l3)
    e_pos = jnp.exp(cum)
    e_neg = jnp.exp(-cum)
    e_prev = jnp.exp(cum - lw)
    r_t = r * e_pos
    k_t = km * e_neg
    rkm = r * km

    for h in range(H):
        sl = slice(h * N, (h + 1) * N)
        kn = kn_ref[:, sl]
        kn = kn / jnp.maximum(jnp.sqrt(jnp.sum(kn * kn, axis=-1, keepdims=True)), 1e-12)
        x_ref[h, :C, :] = (-kn * e_prev[:, sl]).astype(BF16)
        x_ref[h, C:, :] = r_t[:, sl].astype(BF16)
        z_ref[h, :C, :] = k_t[:, sl].astype(BF16)
        z_ref[h, C:, :] = (kn * a[:, sl] * e_neg[:, sl]).astype(BF16)
        vh_ref[h] = v[:, sl]
        rkm_ref[h] = rkm[:, sl]

    X = x_ref[...]
    Z = z_ref[...]
    V = vh_ref[...]
    Vb = V.astype(BF16)
    S = state_ref[...]

    A = _bmm_nt(X, Z)
    ri = lax.broadcasted_iota(jnp.int32, (2 * C, 2 * C), 0)
    ci = lax.broadcasted_iota(jnp.int32, (2 * C, 2 * C), 1)
    ri_m = jnp.where(ri >= C, ri - C, ri)
    ci_m = jnp.where(ci >= C, ci - C, ci)
    keep = ci_m < ri_m + jnp.where(ri >= C, 1, 0)
    A = jnp.where(keep[None], A, 0.0)
    A_ak = A[:, :C, :C]
    L = A[:, :C, C:]
    M_r = A[:, C:, :].astype(BF16)

    eye = (lax.broadcasted_iota(jnp.int32, (C, C), 0) == lax.broadcasted_iota(jnp.int32, (C, C), 1)).astype(F32)
    T = eye[None] + L
    P = L
    span = 1
    while 2 * span < C:
        Pb = P.astype(BF16)
        P = _bmm(Pb, Pb)
        T = T + _bmm(T.astype(BF16), P.astype(BF16))
        span *= 2

    XS = _bmm_nt(X, S.astype(BF16))
    W = XS[:, :C] + _bmm(A_ak.astype(BF16), Vb)
    U = _bmm(T.astype(BF16), W.astype(BF16))
    VU = jnp.concatenate([V, U], axis=1)
    VUb = VU.astype(BF16)
    Y = XS[:, C:] + _bmm(M_r, VUb)
    VUt = jnp.swapaxes(VU, 1, 2).astype(BF16)
    S_add = _bmm(VUt, Z)

    outs = []
    for h in range(H):
        sl = slice(h * N, (h + 1) * N)
        state_ref[h] = (S[h] + S_add[h]) * e_pos[C - 1:C, sl]
        y = Y[h]
        mean = jnp.mean(y, axis=-1, keepdims=True)
        var = jnp.mean(jnp.square(y - mean), axis=-1, keepdims=True)
        yn = (y - mean) * lax.rsqrt(var + GN_EPS) * lnw_ref[h] + lnb_ref[h]
        bonus = jnp.sum(rkm_ref[h] * rk_ref[h], axis=-1, keepdims=True) * V[h]
        outs.append(yn + bonus)
    o_ref[...] = (jnp.concatenate(outs, axis=-1) * g_ref[...]).astype(o_ref.dtype)


def _rwkv_chunk(r, lw, km, v, kn, a, g, r_k, lnx_w, lnx_b, batch, seq):
    m, c = r.shape
    C, H, N = CHUNK, RWKV_HEADS, HEAD_DIM
    per_b = seq // C
    row = pl.BlockSpec((C, c), lambda b, t: (b * per_b + t, 0))
    hv = pl.BlockSpec((H, 1, N), lambda b, t: (0, 0, 0))
    tri = (jnp.arange(C)[:, None] >= jnp.arange(C)[None, :]).astype(BF16)
    return pl.pallas_call(
        _rwkv_chunk_kernel,
        grid=(batch, per_b),
        in_specs=[row] * 7 + [hv, hv, hv, pl.BlockSpec((C, C), lambda b, t: (0, 0))],
        out_specs=row,
        out_shape=jax.ShapeDtypeStruct((m, c), BF16),
        scratch_shapes=[pltpu.VMEM((H, N, N), F32),
                        pltpu.VMEM((H, 2 * C, N), BF16), pltpu.VMEM((H, 2 * C, N), BF16),
                        pltpu.VMEM((H, C, N), F32), pltpu.VMEM((H, C, N), F32)],
        compiler_params=_params(("parallel", "arbitrary")),
    )(r, lw, km, v, kn, a, g, r_k.reshape(H, 1, N), lnx_w.reshape(H, 1, N), lnx_b.reshape(H, 1, N), tri)


def _mix_kernel(o1, o2, o3, l1, l2, l3, rw_ref, gate_ref, bg_ref, x_ref, gt_ref, wa_ref, wr_ref, wo_ref, out_ref):
    la, lb, lc = l1[...], l2[...], l3[...]
    mx = jnp.maximum(jnp.maximum(la, lb), lc)
    ea, eb, ec = jnp.exp(la - mx), jnp.exp(lb - mx), jnp.exp(lc - mx)
    att = (ea * o1[...] + eb * o2[...] + ec * o3[...]) / (ea + eb + ec)
    y_att = _dot(att.astype(BF16), wa_ref[...])
    y_rwkv = _dot(rw_ref[...], wr_ref[...])
    gates = jax.nn.sigmoid(gate_ref[...] + bg_ref[...])
    d = y_att.shape[1]
    mix = gates[:, :d] * y_att + gates[:, d:] * y_rwkv
    out_ref[...] = x_ref[...] + gt_ref[0] * _dot(mix.astype(BF16), wo_ref[...])


def _mix(att_o, att_l, rw, gate, b_gate, x2, gt1, wa, wr, wo, seq, tm):
    m, d = x2.shape
    per_b = seq // tm
    row = lambda w: pl.BlockSpec((tm, w), lambda i: (i, 0))
    full = lambda a: pl.BlockSpec(a.shape, lambda i: (0, 0))
    return pl.pallas_call(
        _mix_kernel,
        grid=(m // tm,),
        in_specs=[row(ATT_WIDTH)] * 6 + [row(d), row(2 * d), full(b_gate), row(d),
                                         pl.BlockSpec((1, 1, d), lambda i: (i // per_b, 0, 0)),
                                         full(wa), full(wr), full(wo)],
        out_specs=row(d),
        out_shape=jax.ShapeDtypeStruct((m, d), F32),
        compiler_params=_params(("parallel",)),
    )(*att_o, *att_l, rw, gate, b_gate, x2, gt1, wa, wr, wo)


HALO = 16


def _ffn_kernel(x_ref, xh_ref, nw_ref, sc_ref, sh_ref, gt_ref, wg_ref, wv_ref, cwg_ref, cwv_ref, cbg_ref, cbv_ref,
                wd_ref, nf_ref, o_ref, h_ref, acc_ref, *, per_b):
    i = pl.program_id(0)
    j = pl.program_id(1)
    tm = x_ref.shape[0]

    @pl.when(j == 0)
    def _():
        h_ref[HALO:, :] = _adaln(x_ref[...], nw_ref[...], sc_ref[0], sh_ref[0]).astype(BF16)
        hh = _adaln(xh_ref[...], nw_ref[...], sc_ref[0], sh_ref[0])
        h_ref[:HALO, :] = jnp.where(i % per_b == 0, 0.0, hh).astype(BF16)
        acc_ref[...] = jnp.zeros_like(acc_ref)

    h = h_ref[...]

    def conv(u, cw_ref, cb_ref):
        out = cb_ref[...] + cw_ref[2:3, :] * u[HALO:, :]
        out = out + cw_ref[1:2, :] * u[HALO - 1:HALO - 1 + tm, :]
        return out + cw_ref[0:1, :] * u[HALO - 2:HALO - 2 + tm, :]

    gate = conv(_dot(h, wg_ref[...]), cwg_ref, cbg_ref)
    val = conv(_dot(h, wv_ref[...]), cwv_ref, cbv_ref)
    act = (gate * jax.nn.sigmoid(gate) * val).astype(BF16)
    acc_ref[...] += _dot(act, wd_ref[...])

    @pl.when(j == pl.num_programs(1) - 1)
    def _():
        x2 = x_ref[...] + gt_ref[0] * acc_ref[...]
        y = x2 * lax.rsqrt(jnp.mean(x2 * x2, axis=-1, keepdims=True) + RMS_EPS)
        o_ref[...] = y * nf_ref[...]


def _ffn(x1, nw, sc, sh, gt, w_up, conv_w, conv_b, w_down, nf, seq, tm, tf):
    m, d = x1.shape
    f = w_down.shape[0]
    nj = f // tf
    per_b = seq // tm
    hb = tm // HALO
    rowc = lambda i, j: (i, 0)
    bvec = pl.BlockSpec((1, 1, d), lambda i, j: (i // per_b, 0, 0))
    conv_b2 = conv_b.reshape(1, 2 * f)
    return pl.pallas_call(
        functools.partial(_ffn_kernel, per_b=per_b),
        grid=(m // tm, nj),
        in_specs=[pl.BlockSpec((tm, d), rowc),
                  pl.BlockSpec((HALO, d), lambda i, j: (jnp.maximum(i * hb - 1, 0), 0)),
                  pl.BlockSpec((1, d), lambda i, j: (0, 0)), bvec, bvec, bvec,
                  pl.BlockSpec((d, tf), lambda i, j: (0, j)),
                  pl.BlockSpec((d, tf), lambda i, j: (0, nj + j)),
                  pl.BlockSpec((CONV_WIDTH, tf), lambda i, j: (0, j)),
                  pl.BlockSpec((CONV_WIDTH, tf), lambda i, j: (0, nj + j)),
                  pl.BlockSpec((1, tf), lambda i, j: (0, j)),
                  pl.BlockSpec((1, tf), lambda i, j: (0, nj + j)),
                  pl.BlockSpec((tf, d), lambda i, j: (j, 0)),
                  pl.BlockSpec((1, d), lambda i, j: (0, 0))],
        out_specs=pl.BlockSpec((tm, d), rowc),
        out_shape=jax.ShapeDtypeStruct((m, d), F32),
        scratch_shapes=[pltpu.VMEM((tm + HALO, d), BF16), pltpu.VMEM((tm, d), F32)],
        compiler_params=_params(("parallel", "arbitrary")),
    )(x1, x1, nw, sc, sh, gt, w_up, w_up, conv_w, conv_w, conv_b2, conv_b2, w_down, nf)


def kernel(x, c, w_ada, b_ada, norm1_w, w_in, b_gate, mu_shift, w0, w2, a0, a2, g2, k_k, k_a, r_k, lnx_w, lnx_b,
           w_att_out, w_rwkv_out, w_o, norm2_w, w_up, conv_w, conv_b, w_down, norm_f_w):
    batch, seq, d = x.shape
    depth = w_ada.shape[0]
    att_in = len(ATT_PATTERNS) * 3 * ATT_WIDTH
    lora = DECAY_LORA + AAA_LORA + GATE_LORA
    xf = x.reshape(batch * seq, d)
    for l in range(depth):
        ada = _ada(c, w_ada[l], b_ada[l])
        sh1, sc1, gt1, sh2, sc2, gt2 = [t.reshape(batch, 1, d) for t in jnp.split(ada, 6, axis=-1)]
        win = w_in[l].astype(BF16)
        w_att = win[:, :att_in]
        w_rkv = win[:, att_in:att_in + 3 * d]
        w_lora = jnp.pad(win[:, att_in + 3 * d:att_in + 3 * d + lora], ((0, 0), (0, LORA_PAD - lora)))
        w_gate = win[:, att_in + 3 * d + lora:]
        nw1 = norm1_w[l].reshape(1, d)

        proj_att = _norm_mm(xf, nw1, sc1, sh1, w_att, BF16, 1024, 512, seq)
        z_main = _norm_mm(xf, nw1, sc1, sh1, w_rkv, F32, 1024, 512, seq)
        z_lora = _norm_mm(xf, nw1, sc1, sh1, w_lora, F32, 1024, LORA_PAD, seq)
        gate = _norm_mm(xf, nw1, sc1, sh1, w_gate, F32, 1024, 512, seq)

        pa = proj_att.reshape(batch, seq, att_in)
        att_o, att_l = [], []
        for g, (_, dilation) in enumerate(ATT_PATTERNS):
            o, lse = _attention_group(pa, g, dilation, batch, seq)
            att_o.append(o.reshape(batch * seq, ATT_WIDTH))
            att_l.append(lse.reshape(batch * seq, ATT_WIDTH))

        mu = mu_shift[l]
        mu_lo = jnp.pad(mu[3 * d:], (0, LORA_PAD - lora)).reshape(1, LORA_PAD)
        r, lw, km, v, kn, a, g_out = _rwkv_prep(
            z_main, z_lora, mu[:3 * d].reshape(1, 3 * d), mu_lo, w0[l].reshape(1, d), w2[l].astype(BF16),
            a0[l].reshape(1, d), a2[l].astype(BF16), g2[l].astype(BF16), k_k[l].reshape(1, d), k_a[l].reshape(1, d),
            batch, seq, 256)
        rw = _rwkv_chunk(r, lw, km, v, kn, a, g_out, r_k[l], lnx_w[l], lnx_b[l], batch, seq)

        x1 = _mix(att_o, att_l, rw, gate, b_gate[l].reshape(1, 2 * d), xf, gt1,
                  w_att_out[l].astype(BF16), w_rwkv_out[l].astype(BF16), w_o[l].astype(BF16), seq, 512)
        xf = _ffn(x1, norm2_w[l].reshape(1, d), sc2, sh2, gt2, w_up[l].astype(BF16), conv_w[l], conv_b[l],
                  w_down[l].astype(BF16), norm_f_w.reshape(1, d), seq, 512, 1408)
    return xf.reshape(batch, seq, d)
```

```python
import functools

import jax
import jax.numpy as jnp
from jax import lax
from jax.experimental import pallas as pl
from jax.experimental.pallas import tpu as pltpu

F32 = jnp.float32
BF16 = jnp.bfloat16

LANES = 128
ATT_PATTERNS = ((128, 1), (512, 4), (2048, 16))
ATT_HEADS = 8
HEAD_DIM = 64
ATT_WIDTH = ATT_HEADS * HEAD_DIM
ATT_BLOCK = 128
ATT_TQ = 512
RWKV_HEADS = 16
DECAY_LORA = 64
AAA_LORA = 64
GATE_LORA = 160
LORA_PAD = 512
CONV_WIDTH = 3
RMS_EPS = 1e-6
GN_EPS = 64e-5
CHUNK = 64
PREP_ROWS = 256
VMEM_LIMIT = 48 * 1024 * 1024


def _dot(a, b):
    return jnp.dot(a, b, preferred_element_type=F32)


def _dot_nt(a, b):
    return lax.dot_general(a, b, (((1,), (1,)), ((), ())), preferred_element_type=F32)


def _split3(x):
    hi = x.astype(BF16)
    r1 = x - hi.astype(F32)
    mid = r1.astype(BF16)
    lo = (r1 - mid.astype(F32)).astype(BF16)
    return hi, mid, lo


def _dot3(x, w):
    hi, mid, lo = _split3(x)
    return _dot(hi, w) + _dot(mid, w) + _dot(lo, w)


def _params(sem):
    return pltpu.CompilerParams(dimension_semantics=sem, vmem_limit_bytes=VMEM_LIMIT)


def _ada_kernel(c_ref, w_ref, b_ref, o_ref):
    ch, cl, _ = _split3(c_ref[...])
    wh, wl, _ = _split3(w_ref[...])
    o_ref[...] = _dot(ch, wh) + _dot(cl, wh) + _dot(ch, wl) + b_ref[...]


def _ada(c, w_ada, b_ada):
    b, d = c.shape
    n = w_ada.shape[1]
    tn = 1536
    return pl.pallas_call(
        _ada_kernel,
        grid=(n // tn,),
        in_specs=[pl.BlockSpec((b, d), lambda j: (0, 0)),
                  pl.BlockSpec((d, tn), lambda j: (0, j)),
                  pl.BlockSpec((1, tn), lambda j: (0, j))],
        out_specs=pl.BlockSpec((b, tn), lambda j: (0, j)),
        out_shape=jax.ShapeDtypeStruct((b, n), F32),
        compiler_params=_params(("arbitrary",)),
    )(c, w_ada, b_ada.reshape(1, n))


def _adaln(x, nw, sc, sh):
    y = x * lax.rsqrt(jnp.mean(x * x, axis=-1, keepdims=True) + RMS_EPS)
    return (y * nw) * (1.0 + sc) + sh


def _norm_kernel(x_ref, nw_ref, sc_ref, sh_ref, o_ref):
    o_ref[...] = _adaln(x_ref[...], nw_ref[...], sc_ref[0], sh_ref[0]).astype(o_ref.dtype)


def _norm(x2, nw, sc, sh, seq, tm):
    m, d = x2.shape
    per_b = seq // tm
    bvec = pl.BlockSpec((1, 1, d), lambda i: (i // per_b, 0, 0))
    return pl.pallas_call(
        _norm_kernel,
        grid=(m // tm,),
        in_specs=[pl.BlockSpec((tm, d), lambda i: (i, 0)), pl.BlockSpec((1, d), lambda i: (0, 0)), bvec, bvec],
        out_specs=pl.BlockSpec((tm, d), lambda i: (i, 0)),
        out_shape=jax.ShapeDtypeStruct((m, d), BF16),
        compiler_params=_params(("parallel",)),
    )(x2, nw, sc, sh)


def _mm_kernel(a_ref, w_ref, o_ref):
    o_ref[...] = _dot(a_ref[...], w_ref[...]).astype(o_ref.dtype)


def _mm(a, w, out_dtype, tm, tn):
    m, k = a.shape
    n = w.shape[1]
    return pl.pallas_call(
        _mm_kernel,
        grid=(m // tm, n // tn),
        in_specs=[pl.BlockSpec((tm, k), lambda i, j: (i, 0)), pl.BlockSpec((k, tn), lambda i, j: (0, j))],
        out_specs=pl.BlockSpec((tm, tn), lambda i, j: (i, j)),
        out_shape=jax.ShapeDtypeStruct((m, n), out_dtype),
        compiler_params=_params(("parallel", "arbitrary")),
    )(a, w)


def _mm_dilate_kernel(a_ref, w_ref, o_ref, acc_ref, *, dilation):
    tm = a_ref.shape[0]
    rows = tm // dilation
    if dilation == 1:
        o_ref[0, 0] = _dot(a_ref[...], w_ref[...]).astype(o_ref.dtype)
    else:
        acc = _dot(a_ref[...], w_ref[...])
        nct = acc.shape[1] // LANES
        for c in range(nct):
            acc_ref[c] = acc[:, c * LANES:(c + 1) * LANES]
        for r in range(dilation):
            o_ref[0, r] = jnp.concatenate(
                [acc_ref[c, pl.ds(r, rows, stride=dilation), :] for c in range(nct)], axis=-1).astype(o_ref.dtype)


def _mm_dilate(a, w, dilation, batch, seq, tm, tn):
    m, k = a.shape
    n = w.shape[1]
    per_b = seq // tm
    rows = tm // dilation
    return pl.pallas_call(
        functools.partial(_mm_dilate_kernel, dilation=dilation),
        grid=(m // tm, n // tn),
        in_specs=[pl.BlockSpec((tm, k), lambda i, j: (i, 0)), pl.BlockSpec((k, tn), lambda i, j: (0, j))],
        out_specs=pl.BlockSpec((1, dilation, rows, tn), lambda i, j: (i // per_b, 0, i % per_b, j)),
        out_shape=jax.ShapeDtypeStruct((batch, dilation, seq // dilation, n), BF16),
        scratch_shapes=[pltpu.VMEM((tn // LANES, tm, LANES), F32)],
        compiler_params=_params(("parallel", "arbitrary")),
    )(a, w)


def _att_kernel(q_ref, k_ref, kh_ref, v_ref, vh_ref, o_ref, l_ref, *, blocks_per_seq):
    i = pl.program_id(0)
    nq = q_ref.shape[0] // ATT_BLOCK
    qi = lax.broadcasted_iota(jnp.int32, (ATT_BLOCK, 2 * ATT_BLOCK), 0)
    kj = lax.broadcasted_iota(jnp.int32, (ATT_BLOCK, 2 * ATT_BLOCK), 1)
    band = (kj >= qi) & (kj <= qi + ATT_BLOCK)
    neg = jnp.float32(-1e30)
    bias_std = jnp.where(band, 0.0, neg)
    bias_first = jnp.where(band & (kj >= ATT_BLOCK), 0.0, neg)
    lane = lax.broadcasted_iota(jnp.int32, (ATT_BLOCK, LANES), 1)
    o_blocks, l_blocks = [], []
    for qb in range(nq):
        rows = slice(qb * ATT_BLOCK, (qb + 1) * ATT_BLOCK)
        prev = slice((qb - 1) * ATT_BLOCK, qb * ATT_BLOCK)
        first = (i * nq + qb) % blocks_per_seq == 0
        bias = jnp.where(first, bias_first, bias_std)
        q = q_ref[rows, :]
        k = jnp.concatenate([kh_ref[...] if qb == 0 else k_ref[prev, :], k_ref[rows, :]], axis=0)
        v = jnp.concatenate([vh_ref[...] if qb == 0 else v_ref[prev, :], v_ref[rows, :]], axis=0)
        outs = []
        lse_tile = jnp.zeros((ATT_BLOCK, LANES), F32)
        for h in range(ATT_HEADS):
            sl = slice(h * HEAD_DIM, (h + 1) * HEAD_DIM)
            s = _dot_nt(q[:, sl], k[:, sl]) + bias
            m = jnp.max(s, axis=-1, keepdims=True)
            p = jnp.exp(s - m)
            den = jnp.sum(p, axis=-1, keepdims=True)
            outs.append(_dot(p.astype(BF16), v[:, sl]) / den)
            lse_tile = jnp.where(lane == h, m + jnp.log(den), lse_tile)
        o_blocks.append(jnp.concatenate(outs, axis=-1).astype(o_ref.dtype))
        l_blocks.append(lse_tile)
    o_ref[...] = jnp.concatenate(o_blocks, axis=0)
    l_ref[...] = jnp.concatenate(l_blocks, axis=0)


def _attention(qkv, blocks_per_seq):
    rows = qkv.shape[0]
    tq = ATT_TQ
    hb = tq // ATT_BLOCK
    main = lambda c: pl.BlockSpec((tq, ATT_WIDTH), lambda i: (i, c))
    halo = lambda c: pl.BlockSpec((ATT_BLOCK, ATT_WIDTH), lambda i: (jnp.maximum(i * hb - 1, 0), c))
    return pl.pallas_call(
        functools.partial(_att_kernel, blocks_per_seq=blocks_per_seq),
        grid=(rows // tq,),
        in_specs=[main(0), main(1), halo(1), main(2), halo(2)],
        out_specs=[pl.BlockSpec((tq, ATT_WIDTH), lambda i: (i, 0)), pl.BlockSpec((tq, LANES), lambda i: (i, 0))],
        out_shape=[jax.ShapeDtypeStruct((rows, ATT_WIDTH), BF16), jax.ShapeDtypeStruct((rows, LANES), F32)],
        compiler_params=_params(("parallel",)),
    )(qkv, qkv, qkv, qkv, qkv)


def _rwkv_prep_kernel(zr_ref, zk_ref, zv_ref, zl_ref, mu_ref, mul_ref, w0_ref, w2_ref, a0_ref, a2_ref, g2_ref,
                      kk_ref, ka_ref, rk_ref, tri_ref, hsum_ref, hexp_ref,
                      xa_out, xr_out, zk_out, zb_out, v_out, bonus_out, g_out, glast_out,
                      cr_ref, ck_ref, cv_ref, cl_ref):
    @pl.when(pl.program_id(1) == 0)
    def _():
        for c in (cr_ref, ck_ref, cv_ref, cl_ref):
            c[...] = jnp.zeros_like(c)

    tm = zr_ref.shape[0]
    d = zr_ref.shape[1]

    def shift(z_ref, carry_ref, mu):
        z = z_ref[...]
        rows = lax.broadcasted_iota(jnp.int32, z.shape, 0)
        prev = jnp.where(rows == 0, carry_ref[...], pltpu.roll(z, 1, axis=0))
        carry_ref[...] = z[tm - 1:tm, :]
        return z + (prev - z) * mu

    r = shift(zr_ref, cr_ref, mu_ref[:, :d])
    k = shift(zk_ref, ck_ref, mu_ref[:, d:2 * d])
    v = shift(zv_ref, cv_ref, mu_ref[:, 2 * d:])
    zl = shift(zl_ref, cl_ref, mul_ref[...])
    w_low = zl[:, :DECAY_LORA]
    a_low = zl[:, DECAY_LORA:DECAY_LORA + AAA_LORA]
    g_low = zl[:, DECAY_LORA + AAA_LORA:DECAY_LORA + AAA_LORA + GATE_LORA]

    wpre = w0_ref[...] + _dot(jnp.tanh(w_low).astype(BF16), w2_ref[...])
    lw = -jnp.exp(-jax.nn.softplus(-wpre) - 0.5)
    a = jax.nn.sigmoid(a0_ref[...] + _dot(a_low.astype(BF16), a2_ref[...]))
    g = _dot(jax.nn.sigmoid(g_low).astype(BF16), g2_ref[...])
    km = k * (1.0 + (a - 1.0) * ka_ref[...])

    def head_sum(x):
        return _dot3(x, hsum_ref[...])

    kn = k * kk_ref[...]
    inv = lax.rsqrt(jnp.maximum(head_sum(kn * kn), 1e-24))
    kn = kn * _dot3(inv, hexp_ref[...])
    bonus = _dot3(head_sum(r * km * rk_ref[...]), hexp_ref[...]) * v

    tri = tri_ref[...]
    for cidx in range(tm // CHUNK):
        rows = slice(cidx * CHUNK, (cidx + 1) * CHUNK)
        lwc = lw[rows]
        l1, l2, l3 = _split3(lwc)
        cum = _dot(tri, l1) + _dot(tri, l2) + _dot(tri, l3)
        e_pos = jnp.exp(cum)
        e_neg = jnp.exp(-cum)
        knc = kn[rows]
        xa_out[rows, :] = (-knc * jnp.exp(cum - lwc)).astype(BF16)
        xr_out[rows, :] = (r[rows] * e_pos).astype(BF16)
        zk_out[rows, :] = (km[rows] * e_neg).astype(BF16)
        zb_out[rows, :] = (knc * a[rows] * e_neg).astype(BF16)
        glast_out[cidx] = e_pos[CHUNK - 1:CHUNK, :]
    v_out[...] = v.astype(BF16)
    bonus_out[...] = bonus.astype(BF16)
    g_out[...] = g.astype(BF16)


def _rwkv_prep(zall, col0, mu, mu_lo, w0, w2, a0, a2, g2, k_k, k_a, r_k, batch, seq):
    m = zall.shape[0]
    d = w0.shape[1]
    tm = PREP_ROWS
    per_b = seq // tm
    cb = col0 // d
    row = lambda b, t: (b * per_b + t, 0)
    fixed = lambda b, t: (0, 0)
    zspec = lambda c: pl.BlockSpec((tm, d), lambda b, t: (b * per_b + t, cb + c))
    lspec = pl.BlockSpec((tm, LORA_PAD), lambda b, t: (b * per_b + t, (col0 + 3 * d) // LORA_PAD))
    vec = pl.BlockSpec((1, d), fixed)
    tri = (jnp.arange(CHUNK)[:, None] >= jnp.arange(CHUNK)[None, :]).astype(BF16)
    head_of = jnp.arange(d) // HEAD_DIM
    hsum = (head_of[:, None] == jnp.arange(LANES)[None, :]).astype(BF16)
    hexp = hsum.T
    big = jax.ShapeDtypeStruct((m, d), BF16)
    return pl.pallas_call(
        _rwkv_prep_kernel,
        grid=(batch, per_b),
        in_specs=[zspec(0), zspec(1), zspec(2), lspec,
                  pl.BlockSpec((1, 3 * d), fixed), pl.BlockSpec((1, LORA_PAD), fixed),
                  vec, pl.BlockSpec((DECAY_LORA, d), fixed),
                  vec, pl.BlockSpec((AAA_LORA, d), fixed),
                  pl.BlockSpec((GATE_LORA, d), fixed), vec, vec, vec,
                  pl.BlockSpec((CHUNK, CHUNK), fixed), pl.BlockSpec((d, LANES), fixed),
                  pl.BlockSpec((LANES, d), fixed)],
        out_specs=[pl.BlockSpec((tm, d), row)] * 7 +
                  [pl.BlockSpec((tm // CHUNK, 1, d), lambda b, t: (b * per_b + t, 0, 0))],
        out_shape=[big] * 7 + [jax.ShapeDtypeStruct((m // CHUNK, 1, d), F32)],
        scratch_shapes=[pltpu.VMEM((1, d), F32)] * 3 + [pltpu.VMEM((1, LORA_PAD), F32)],
        compiler_params=_params(("parallel", "arbitrary")),
    )(zall, zall, zall, zall, mu, mu_lo, w0, w2, a0, a2, g2, k_k, k_a, r_k, tri, hsum, hexp)


def _rwkv_chunk_kernel(xa_ref, xr_ref, zk_ref, zb_ref, v_ref, bonus_ref, g_ref, glast_ref, lnw_ref, lnb_ref,
                       o_ref, state_ref):
    C = CHUNK
    N = HEAD_DIM

    @pl.when(pl.program_id(1) == 0)
    def _():
        state_ref[...] = jnp.zeros_like(state_ref)

    ri = lax.broadcasted_iota(jnp.int32, (2 * C, 2 * C), 0)
    ci = lax.broadcasted_iota(jnp.int32, (2 * C, 2 * C), 1)
    ri_m = jnp.where(ri >= C, ri - C, ri)
    ci_m = jnp.where(ci >= C, ci - C, ci)
    keep = ci_m < ri_m + jnp.where(ri >= C, 1, 0)
    eye = (lax.broadcasted_iota(jnp.int32, (C, C), 0) == lax.broadcasted_iota(jnp.int32, (C, C), 1)).astype(F32)
    right = lax.broadcasted_iota(jnp.int32, (C, 2 * C), 1) >= C
    glast = glast_ref[0]

    heads = range(RWKV_HEADS)
    each = lambda f: [f(h) for h in heads]
    sl = lambda h: slice(h * N, (h + 1) * N)
    xa, xr, zk, zb, vv = xa_ref[...], xr_ref[...], zk_ref[...], zb_ref[...], v_ref[...]
    S_all = state_ref[...]
    X = each(lambda h: jnp.concatenate([xa[:, sl(h)], xr[:, sl(h)]], axis=0))
    Z = each(lambda h: jnp.concatenate([zk[:, sl(h)], zb[:, sl(h)]], axis=0))
    Vb = each(lambda h: vv[:, sl(h)])

    A = each(lambda h: jnp.where(keep, _dot_nt(X[h], Z[h]), 0.0))
    Lb = each(lambda h: A[h][:C, C:].astype(BF16))
    Q = each(lambda h: jnp.concatenate([_dot(Lb[h], Lb[h]), eye + A[h][:C, C:]], axis=1))
    span = 2
    while 2 * span < C:
        Qb = each(lambda h: Q[h].astype(BF16))
        Q = each(lambda h: _dot(Qb[h][:, :C], Qb[h]) + jnp.where(right, Q[h], 0.0))
        span *= 2
    T = each(lambda h: Q[h][:, C:] + _dot(Q[h][:, :C].astype(BF16), Q[h][:, C:].astype(BF16)))

    XS = each(lambda h: _dot_nt(X[h], S_all[h].astype(BF16)))
    AV = each(lambda h: _dot(A[h][:, :C].astype(BF16), Vb[h]))
    W = each(lambda h: (XS[h][:C] + AV[h][:C]).astype(BF16))
    U = each(lambda h: _dot(T[h].astype(BF16), W[h]))
    Y = each(lambda h: XS[h][C:] + AV[h][C:] + _dot(A[h][C:, C:].astype(BF16), U[h].astype(BF16)))
    VUt = each(lambda h: jnp.concatenate([Vb[h].astype(F32), U[h]], axis=0).T.astype(BF16))
    state_ref[...] = jnp.stack(each(lambda h: (S_all[h] + _dot(VUt[h], Z[h])) * glast[:, sl(h)]))

    def group_norm(y):
        mean = jnp.mean(y, axis=-1, keepdims=True)
        var = jnp.mean(jnp.square(y - mean), axis=-1, keepdims=True)
        return (y - mean) * lax.rsqrt(var + GN_EPS)

    y = jnp.concatenate(each(lambda h: group_norm(Y[h])), axis=-1) * lnw_ref[...] + lnb_ref[...]
    o_ref[...] = ((y + bonus_ref[...].astype(F32)) * g_ref[...].astype(F32)).astype(o_ref.dtype)


def _rwkv_chunk(xa, xr, zk, zb, v, bonus, g, glast, lnx_w, lnx_b, batch, seq):
    m, c = xa.shape
    C, H, N = CHUNK, RWKV_HEADS, HEAD_DIM
    per_b = seq // C
    row = pl.BlockSpec((C, c), lambda b, t: (b * per_b + t, 0))
    vec = pl.BlockSpec((1, c), lambda b, t: (0, 0))
    return pl.pallas_call(
        _rwkv_chunk_kernel,
        grid=(batch, per_b),
        in_specs=[row] * 7 + [pl.BlockSpec((1, 1, c), lambda b, t: (b * per_b + t, 0, 0)), vec, vec],
        out_specs=row,
        out_shape=jax.ShapeDtypeStruct((m, c), BF16),
        scratch_shapes=[pltpu.VMEM((H, N, N), F32)],
        compiler_params=_params(("parallel", "arbitrary")),
    )(xa, xr, zk, zb, v, bonus, g, glast, lnx_w, lnx_b)


def _mix_kernel(o1, o2, o3, l1, l2, l3, rw_ref, gate_ref, bg_ref, x_ref, gt_ref, hexp_ref, wa_ref, wr_ref, wo_ref,
                out_ref, os_ref, ls_ref):
    tm = x_ref.shape[0]

    def token_order(o_ref, l_ref):
        d = o_ref.shape[1]
        if d == 1:
            return o_ref[0, 0].astype(F32), l_ref[0, 0]
        rows = tm // d
        nct = os_ref.shape[0]
        for r in range(d):
            o = o_ref[0, r].astype(F32)
            for c in range(nct):
                os_ref[c, pl.ds(r, rows, stride=d), :] = o[:, c * LANES:(c + 1) * LANES]
            ls_ref[pl.ds(r, rows, stride=d), :] = l_ref[0, r]
        return jnp.concatenate([os_ref[c] for c in range(nct)], axis=-1), ls_ref[...]

    oa, la = token_order(o1, l1)
    ob, lb = token_order(o2, l2)
    oc, lc = token_order(o3, l3)
    mx = jnp.maximum(jnp.maximum(la, lb), lc)
    ea, eb, ec = jnp.exp(la - mx), jnp.exp(lb - mx), jnp.exp(lc - mx)
    inv = 1.0 / (ea + eb + ec)
    hexp = hexp_ref[...]
    att = (_dot3(ea * inv, hexp) * oa + _dot3(eb * inv, hexp) * ob + _dot3(ec * inv, hexp) * oc)
    y_att = _dot(att.astype(BF16), wa_ref[...])
    y_rwkv = _dot(rw_ref[...], wr_ref[...])
    gates = jax.nn.sigmoid(gate_ref[...] + bg_ref[...])
    d = y_att.shape[1]
    mix = gates[:, :d] * y_att + gates[:, d:] * y_rwkv
    out_ref[...] = x_ref[...] + gt_ref[0] * _dot(mix.astype(BF16), wo_ref[...])


def _mix(att_o, att_l, rw, zall, b_gate, x2, gt1, wa, wr, wo, batch, seq, tm):
    m, d = x2.shape
    per_b = seq // tm
    row = lambda w: pl.BlockSpec((tm, w), lambda i: (i, 0))
    full = lambda a: pl.BlockSpec(a.shape, lambda i: (0, 0))

    def dil(arr):
        dd = arr.shape[1]
        return pl.BlockSpec((1, dd, tm // dd, arr.shape[3]), lambda i: (i // per_b, 0, i % per_b, 0))

    hexp = (jnp.arange(LANES)[:, None] == (jnp.arange(ATT_WIDTH) // HEAD_DIM)[None, :]).astype(BF16)
    return pl.pallas_call(
        _mix_kernel,
        grid=(m // tm,),
        in_specs=[dil(a) for a in att_o] + [dil(a) for a in att_l] +
                 [row(d), row(2 * d), full(b_gate), row(d),
                  pl.BlockSpec((1, 1, d), lambda i: (i // per_b, 0, 0)), full(hexp), full(wa), full(wr), full(wo)],
        out_specs=row(d),
        out_shape=jax.ShapeDtypeStruct((m, d), F32),
        scratch_shapes=[pltpu.VMEM((ATT_WIDTH // LANES, tm, LANES), F32), pltpu.VMEM((tm, LANES), F32)],
        compiler_params=_params(("parallel",)),
    )(*att_o, *att_l, rw, zall, b_gate, x2, gt1, hexp, wa, wr, wo)


HALO = 16


def _ffn_kernel(x_ref, xh_ref, nw_ref, sc_ref, sh_ref, gt_ref, wg_ref, wv_ref, cwg_ref, cwv_ref, cbg_ref, cbv_ref,
                wd_ref, nf_ref, o_ref, h_ref, acc_ref, *, per_b):
    i = pl.program_id(0)
    j = pl.program_id(1)
    tm = x_ref.shape[0]

    @pl.when(j == 0)
    def _():
        h_ref[HALO:, :] = _adaln(x_ref[...], nw_ref[...], sc_ref[0], sh_ref[0]).astype(BF16)
        hh = _adaln(xh_ref[...], nw_ref[...], sc_ref[0], sh_ref[0])
        h_ref[:HALO, :] = jnp.where(i % per_b == 0, 0.0, hh).astype(BF16)
        acc_ref[...] = jnp.zeros_like(acc_ref)

    h = h_ref[...]

    def conv(u, cw_ref, cb_ref):
        out = cb_ref[...] + cw_ref[2:3, :] * u[HALO:, :]
        out = out + cw_ref[1:2, :] * u[HALO - 1:HALO - 1 + tm, :]
        return out + cw_ref[0:1, :] * u[HALO - 2:HALO - 2 + tm, :]

    gate = conv(_dot(h, wg_ref[...]), cwg_ref, cbg_ref)
    val = conv(_dot(h, wv_ref[...]), cwv_ref, cbv_ref)
    act = (gate * jax.nn.sigmoid(gate) * val).astype(BF16)
    acc_ref[...] += _dot(act, wd_ref[...])

    @pl.when(j == pl.num_programs(1) - 1)
    def _():
        x2 = x_ref[...] + gt_ref[0] * acc_ref[...]
        y = x2 * lax.rsqrt(jnp.mean(x2 * x2, axis=-1, keepdims=True) + RMS_EPS)
        o_ref[...] = y * nf_ref[...]


def _ffn(x1, nw, sc, sh, gt, w_up, conv_w, conv_b, w_down, nf, seq, tm, tf):
    m, d = x1.shape
    f = w_down.shape[0]
    nj = f // tf
    per_b = seq // tm
    hb = tm // HALO
    rowc = lambda i, j: (i, 0)
    bvec = pl.BlockSpec((1, 1, d), lambda i, j: (i // per_b, 0, 0))
    conv_b2 = conv_b.reshape(1, 2 * f)
    return pl.pallas_call(
        functools.partial(_ffn_kernel, per_b=per_b),
        grid=(m // tm, nj),
        in_specs=[pl.BlockSpec((tm, d), rowc),
                  pl.BlockSpec((HALO, d), lambda i, j: (jnp.maximum(i * hb - 1, 0), 0)),
                  pl.BlockSpec((1, d), lambda i, j: (0, 0)), bvec, bvec, bvec,
                  pl.BlockSpec((d, tf), lambda i, j: (0, j)),
                  pl.BlockSpec((d, tf), lambda i, j: (0, nj + j)),
                  pl.BlockSpec((CONV_WIDTH, tf), lambda i, j: (0, j)),
                  pl.BlockSpec((CONV_WIDTH, tf), lambda i, j: (0, nj + j)),
                  pl.BlockSpec((1, tf), lambda i, j: (0, j)),
                  pl.BlockSpec((1, tf), lambda i, j: (0, nj + j)),
                  pl.BlockSpec((tf, d), lambda i, j: (j, 0)),
                  pl.BlockSpec((1, d), lambda i, j: (0, 0))],
        out_specs=pl.BlockSpec((tm, d), rowc),
        out_shape=jax.ShapeDtypeStruct((m, d), F32),
        scratch_shapes=[pltpu.VMEM((tm + HALO, d), BF16), pltpu.VMEM((tm, d), F32)],
        compiler_params=_params(("parallel", "arbitrary")),
    )(x1, x1, nw, sc, sh, gt, w_up, w_up, conv_w, conv_w, conv_b2, conv_b2, w_down, nf)


def kernel(x, c, w_ada, b_ada, norm1_w, w_in, b_gate, mu_shift, w0, w2, a0, a2, g2, k_k, k_a, r_k, lnx_w, lnx_b,
           w_att_out, w_rwkv_out, w_o, norm2_w, w_up, conv_w, conv_b, w_down, norm_f_w):
    batch, seq, d = x.shape
    assert w_ada.shape[0] == 1, "the fused ffn kernel applies the final RMSNorm; one layer only"
    l = 0
    grp = 3 * ATT_WIDTH
    att_in = len(ATT_PATTERNS) * grp
    lora = DECAY_LORA + AAA_LORA + GATE_LORA
    xf = x.reshape(batch * seq, d)

    ada = _ada(c, w_ada[l], b_ada[l])
    sh1, sc1, gt1, sh2, sc2, gt2 = [t.reshape(batch, 1, d) for t in jnp.split(ada, 6, axis=-1)]
    h = _norm(xf, norm1_w[l].reshape(1, d), sc1, sh1, seq, 1024)

    win = w_in[l]
    w_rest = jnp.concatenate(
        [win[:, att_in + 3 * d + lora:], win[:, att_in:att_in + 3 * d + lora],
         jnp.zeros((d, LORA_PAD - lora), F32)], axis=1).astype(BF16)
    zall = _mm(h, w_rest, F32, 1024, 512)

    att_o, att_l = [], []
    for gi, (_, dilation) in enumerate(ATT_PATTERNS):
        wg = win[:, gi * grp:(gi + 1) * grp]
        wg = jnp.concatenate([wg[:, :ATT_WIDTH] * (HEAD_DIM ** -0.5), wg[:, ATT_WIDTH:]], axis=1).astype(BF16)
        qkv = _mm_dilate(h, wg, dilation, batch, seq, 1024, ATT_WIDTH)
        o, lse = _attention(qkv.reshape(batch * seq, grp), seq // dilation // ATT_BLOCK)
        att_o.append(o.reshape(batch, dilation, seq // dilation, ATT_WIDTH))
        att_l.append(lse.reshape(batch, dilation, seq // dilation, LANES))

    mu = mu_shift[l]
    mu_lo = jnp.pad(mu[3 * d:], (0, LORA_PAD - lora)).reshape(1, LORA_PAD)
    vec = lambda t: t.reshape(1, d)
    xa, xr, zk, zb, v, bonus, g_out, glast = _rwkv_prep(
        zall, 2 * d, mu[:3 * d].reshape(1, 3 * d), mu_lo, vec(w0[l]), w2[l].astype(BF16), vec(a0[l]),
        a2[l].astype(BF16), g2[l].astype(BF16), vec(k_k[l]), vec(k_a[l]), vec(r_k[l]), batch, seq)
    rw = _rwkv_chunk(xa, xr, zk, zb, v, bonus, g_out, glast, vec(lnx_w[l]), vec(lnx_b[l]), batch, seq)

    x1 = _mix(att_o, att_l, rw, zall, b_gate[l].reshape(1, 2 * d), xf, gt1,
              w_att_out[l].astype(BF16), w_rwkv_out[l].astype(BF16), w_o[l].astype(BF16), batch, seq, 256)
    out = _ffn(x1, norm2_w[l].reshape(1, d), sc2, sh2, gt2, w_up[l].astype(BF16), conv_w[l], conv_b[l],
               w_down[l].astype(BF16), norm_f_w.reshape(1, d), seq, 512, 1408)
    return out.reshape(batch, seq, d)
```

```python
import functools

import jax
import jax.numpy as jnp
from jax import lax
from jax.experimental import pallas as pl
from jax.experimental.pallas import tpu as pltpu

F32 = jnp.float32
BF16 = jnp.bfloat16

LANES = 128
ATT_PATTERNS = ((128, 1), (512, 4), (2048, 16))
ATT_HEADS = 8
HEAD_DIM = 64
ATT_WIDTH = ATT_HEADS * HEAD_DIM
ATT_BLOCK = 128
ATT_TQ = 512
RWKV_HEADS = 16
DECAY_LORA = 64
AAA_LORA = 64
GATE_LORA = 160
LORA_PAD = 512
CONV_WIDTH = 3
RMS_EPS = 1e-6
GN_EPS = 64e-5
CHUNK = 64
PREP_ROWS = 256
VMEM_LIMIT = 48 * 1024 * 1024


def _dot(a, b):
    return jnp.dot(a, b, preferred_element_type=F32)


def _dot_nt(a, b):
    return lax.dot_general(a, b, (((1,), (1,)), ((), ())), preferred_element_type=F32)


def _split3(x):
    hi = x.astype(BF16)
    r1 = x - hi.astype(F32)
    mid = r1.astype(BF16)
    lo = (r1 - mid.astype(F32)).astype(BF16)
    return hi, mid, lo


def _dot3(x, w):
    hi, mid, lo = _split3(x)
    return _dot(hi, w) + _dot(mid, w) + _dot(lo, w)


def _dot2(x, w):
    hi = x.astype(BF16)
    lo = (x - hi.astype(F32)).astype(BF16)
    return _dot(hi, w) + _dot(lo, w)


def _params(sem):
    return pltpu.CompilerParams(dimension_semantics=sem, vmem_limit_bytes=VMEM_LIMIT)


def _ada_kernel(c_ref, w_ref, b_ref, o_ref):
    ch, cl, _ = _split3(c_ref[...])
    wh, wl, _ = _split3(w_ref[...])
    o_ref[...] = _dot(ch, wh) + _dot(cl, wh) + _dot(ch, wl) + b_ref[...]


def _ada(c, w_ada, b_ada):
    b, d = c.shape
    n = w_ada.shape[1]
    tn = 1536
    return pl.pallas_call(
        _ada_kernel,
        grid=(n // tn,),
        in_specs=[pl.BlockSpec((b, d), lambda j: (0, 0)),
                  pl.BlockSpec((d, tn), lambda j: (0, j)),
                  pl.BlockSpec((1, tn), lambda j: (0, j))],
        out_specs=pl.BlockSpec((b, tn), lambda j: (0, j)),
        out_shape=jax.ShapeDtypeStruct((b, n), F32),
        compiler_params=_params(("arbitrary",)),
    )(c, w_ada, b_ada.reshape(1, n))


def _adaln(x, nw, sc, sh):
    y = x * lax.rsqrt(jnp.mean(x * x, axis=-1, keepdims=True) + RMS_EPS)
    return (y * nw) * (1.0 + sc) + sh


def _norm_kernel(x_ref, nw_ref, sc_ref, sh_ref, o_ref):
    o_ref[...] = _adaln(x_ref[...], nw_ref[...], sc_ref[0], sh_ref[0]).astype(o_ref.dtype)


def _norm(x2, nw, sc, sh, seq, tm):
    m, d = x2.shape
    per_b = seq // tm
    bvec = pl.BlockSpec((1, 1, d), lambda i: (i // per_b, 0, 0))
    return pl.pallas_call(
        _norm_kernel,
        grid=(m // tm,),
        in_specs=[pl.BlockSpec((tm, d), lambda i: (i, 0)), pl.BlockSpec((1, d), lambda i: (0, 0)), bvec, bvec],
        out_specs=pl.BlockSpec((tm, d), lambda i: (i, 0)),
        out_shape=jax.ShapeDtypeStruct((m, d), BF16),
        compiler_params=_params(("parallel",)),
    )(x2, nw, sc, sh)


def _mm_kernel(a_ref, w_ref, o_ref):
    o_ref[...] = _dot(a_ref[...], w_ref[...]).astype(o_ref.dtype)


def _mm(a, w, out_dtype, tm, tn):
    m, k = a.shape
    n = w.shape[1]
    return pl.pallas_call(
        _mm_kernel,
        grid=(m // tm, n // tn),
        in_specs=[pl.BlockSpec((tm, k), lambda i, j: (i, 0)), pl.BlockSpec((k, tn), lambda i, j: (0, j))],
        out_specs=pl.BlockSpec((tm, tn), lambda i, j: (i, j)),
        out_shape=jax.ShapeDtypeStruct((m, n), out_dtype),
        compiler_params=_params(("parallel", "arbitrary")),
    )(a, w)


def _mm_dilate_kernel(a_ref, w_ref, o_ref, acc_ref, *, dilation):
    tm = a_ref.shape[0]
    rows = tm // dilation
    if dilation == 1:
        o_ref[0, 0] = _dot(a_ref[...], w_ref[...]).astype(o_ref.dtype)
    else:
        acc = _dot(a_ref[...], w_ref[...])
        nct = acc.shape[1] // LANES
        for c in range(nct):
            acc_ref[c] = acc[:, c * LANES:(c + 1) * LANES]
        for r in range(dilation):
            o_ref[0, r] = jnp.concatenate(
                [acc_ref[c, pl.ds(r, rows, stride=dilation), :] for c in range(nct)], axis=-1).astype(o_ref.dtype)


def _mm_dilate(a, w, dilation, batch, seq, tm, tn):
    m, k = a.shape
    n = w.shape[1]
    per_b = seq // tm
    rows = tm // dilation
    return pl.pallas_call(
        functools.partial(_mm_dilate_kernel, dilation=dilation),
        grid=(m // tm, n // tn),
        in_specs=[pl.BlockSpec((tm, k), lambda i, j: (i, 0)), pl.BlockSpec((k, tn), lambda i, j: (0, j))],
        out_specs=pl.BlockSpec((1, dilation, rows, tn), lambda i, j: (i // per_b, 0, i % per_b, j)),
        out_shape=jax.ShapeDtypeStruct((batch, dilation, seq // dilation, n), BF16),
        scratch_shapes=[pltpu.VMEM((tn // LANES, tm, LANES), F32)],
        compiler_params=_params(("parallel", "arbitrary")),
    )(a, w)


def _att_kernel(q_ref, k_ref, kh_ref, v_ref, vh_ref, o_ref, l_ref, *, blocks_per_seq):
    i = pl.program_id(0)
    nq = q_ref.shape[0] // ATT_BLOCK
    qi = lax.broadcasted_iota(jnp.int32, (ATT_BLOCK, 2 * ATT_BLOCK), 0)
    kj = lax.broadcasted_iota(jnp.int32, (ATT_BLOCK, 2 * ATT_BLOCK), 1)
    band = (kj >= qi) & (kj <= qi + ATT_BLOCK)
    neg = jnp.float32(-1e30)
    bias_std = jnp.where(band, 0.0, neg)
    bias_first = jnp.where(band & (kj >= ATT_BLOCK), 0.0, neg)
    lane = lax.broadcasted_iota(jnp.int32, (ATT_BLOCK, LANES), 1)
    o_blocks, l_blocks = [], []
    for qb in range(nq):
        rows = slice(qb * ATT_BLOCK, (qb + 1) * ATT_BLOCK)
        prev = slice((qb - 1) * ATT_BLOCK, qb * ATT_BLOCK)
        first = (i * nq + qb) % blocks_per_seq == 0
        bias = jnp.where(first, bias_first, bias_std)
        q = q_ref[rows, :]
        k = jnp.concatenate([kh_ref[...] if qb == 0 else k_ref[prev, :], k_ref[rows, :]], axis=0)
        v = jnp.concatenate([vh_ref[...] if qb == 0 else v_ref[prev, :], v_ref[rows, :]], axis=0)
        outs = []
        lse_tile = jnp.zeros((ATT_BLOCK, LANES), F32)
        for hp in range(ATT_HEADS // 2):
            tile = slice(hp * LANES, (hp + 1) * LANES)
            q2, k2, v2 = q[:, tile], k[:, tile], v[:, tile]
            halves = []
            for side in range(2):
                own = (lane < HEAD_DIM) if side == 0 else (lane >= HEAD_DIM)
                s = _dot_nt(jnp.where(own, q2, jnp.zeros((), q2.dtype)), k2) + bias
                m = jnp.max(s, axis=-1, keepdims=True)
                p = jnp.exp(s - m)
                den = jnp.sum(p, axis=-1, keepdims=True)
                halves.append(_dot(p.astype(BF16), v2) / den)
                lse_tile = jnp.where(lane == 2 * hp + side, m + jnp.log(den), lse_tile)
            outs.append(jnp.where(lane < HEAD_DIM, halves[0], halves[1]))
        o_blocks.append(jnp.concatenate(outs, axis=-1).astype(o_ref.dtype))
        l_blocks.append(lse_tile)
    o_ref[...] = jnp.concatenate(o_blocks, axis=0)
    l_ref[...] = jnp.concatenate(l_blocks, axis=0)


def _attention(qkv, blocks_per_seq):
    rows = qkv.shape[0]
    tq = ATT_TQ
    hb = tq // ATT_BLOCK
    main = lambda c: pl.BlockSpec((tq, ATT_WIDTH), lambda i: (i, c))
    halo = lambda c: pl.BlockSpec((ATT_BLOCK, ATT_WIDTH), lambda i: (jnp.maximum(i * hb - 1, 0), c))
    return pl.pallas_call(
        functools.partial(_att_kernel, blocks_per_seq=blocks_per_seq),
        grid=(rows // tq,),
        in_specs=[main(0), main(1), halo(1), main(2), halo(2)],
        out_specs=[pl.BlockSpec((tq, ATT_WIDTH), lambda i: (i, 0)), pl.BlockSpec((tq, LANES), lambda i: (i, 0))],
        out_shape=[jax.ShapeDtypeStruct((rows, ATT_WIDTH), BF16), jax.ShapeDtypeStruct((rows, LANES), F32)],
        compiler_params=_params(("parallel",)),
    )(qkv, qkv, qkv, qkv, qkv)


def _rwkv_prep_kernel(zr_ref, zk_ref, zv_ref, zl_ref, mu_ref, mul_ref, w0_ref, w2_ref, a0_ref, a2_ref, g2_ref,
                      kk_ref, ka_ref, rk_ref, tri_ref, hsum_ref, hexp_ref,
                      xa_out, xr_out, zk_out, zb_out, v_out, bonus_out, g_out, glast_out,
                      cr_ref, ck_ref, cv_ref, cl_ref):
    @pl.when(pl.program_id(1) == 0)
    def _():
        for c in (cr_ref, ck_ref, cv_ref, cl_ref):
            c[...] = jnp.zeros_like(c)

    tm = zr_ref.shape[0]
    d = zr_ref.shape[1]

    def shift(z_ref, carry_ref, mu):
        z = z_ref[...].astype(F32)
        rows = lax.broadcasted_iota(jnp.int32, z.shape, 0)
        prev = jnp.where(rows == 0, carry_ref[...], pltpu.roll(z, 1, axis=0))
        carry_ref[...] = z[tm - 1:tm, :]
        return z + (prev - z) * mu

    r = shift(zr_ref, cr_ref, mu_ref[:, :d])
    k = shift(zk_ref, ck_ref, mu_ref[:, d:2 * d])
    v = shift(zv_ref, cv_ref, mu_ref[:, 2 * d:])
    zl = shift(zl_ref, cl_ref, mul_ref[...])
    w_low = zl[:, :DECAY_LORA]
    a_low = zl[:, DECAY_LORA:DECAY_LORA + AAA_LORA]
    g_low = zl[:, DECAY_LORA + AAA_LORA:DECAY_LORA + AAA_LORA + GATE_LORA]

    wpre = w0_ref[...] + _dot(jnp.tanh(w_low).astype(BF16), w2_ref[...])
    lw = -jnp.exp(-jax.nn.softplus(-wpre) - 0.5)
    a = jax.nn.sigmoid(a0_ref[...] + _dot(a_low.astype(BF16), a2_ref[...]))
    g = _dot(jax.nn.sigmoid(g_low).astype(BF16), g2_ref[...])
    km = k * (1.0 + (a - 1.0) * ka_ref[...])

    kn = k * kk_ref[...]
    inv = lax.rsqrt(jnp.maximum(_dot2(kn * kn, hsum_ref[...]), 1e-24))
    kn = kn * _dot2(inv, hexp_ref[...])
    rk_sum = _dot((r * km * rk_ref[...]).astype(BF16), hsum_ref[...])
    bonus = _dot(rk_sum.astype(BF16), hexp_ref[...]) * v

    tri = tri_ref[...]
    for cidx in range(tm // CHUNK):
        rows = slice(cidx * CHUNK, (cidx + 1) * CHUNK)
        lwc = lw[rows]
        l1, l2, l3 = _split3(lwc)
        cum = _dot(tri, l1) + _dot(tri, l2) + _dot(tri, l3)
        e_pos = jnp.exp(cum)
        e_neg = jnp.exp(-cum)
        knc = kn[rows]
        xa_out[rows, :] = (-knc * jnp.exp(cum - lwc)).astype(BF16)
        xr_out[rows, :] = (r[rows] * e_pos).astype(BF16)
        zk_out[rows, :] = (km[rows] * e_neg).astype(BF16)
        zb_out[rows, :] = (knc * a[rows] * e_neg).astype(BF16)
        glast_out[cidx] = e_pos[CHUNK - 1:CHUNK, :]
    v_out[...] = v.astype(BF16)
    bonus_out[...] = bonus.astype(BF16)
    g_out[...] = g.astype(BF16)


def _rwkv_prep(zall, col0, mu, mu_lo, w0, w2, a0, a2, g2, k_k, k_a, r_k, batch, seq):
    m = zall.shape[0]
    d = w0.shape[1]
    tm = PREP_ROWS
    per_b = seq // tm
    cb = col0 // d
    row = lambda b, t: (b * per_b + t, 0)
    fixed = lambda b, t: (0, 0)
    zspec = lambda c: pl.BlockSpec((tm, d), lambda b, t: (b * per_b + t, cb + c))
    lspec = pl.BlockSpec((tm, LORA_PAD), lambda b, t: (b * per_b + t, (col0 + 3 * d) // LORA_PAD))
    vec = pl.BlockSpec((1, d), fixed)
    tri = (jnp.arange(CHUNK)[:, None] >= jnp.arange(CHUNK)[None, :]).astype(BF16)
    head_of = jnp.arange(d) // HEAD_DIM
    hsum = (head_of[:, None] == jnp.arange(LANES)[None, :]).astype(BF16)
    hexp = hsum.T
    big = jax.ShapeDtypeStruct((m, d), BF16)
    return pl.pallas_call(
        _rwkv_prep_kernel,
        grid=(batch, per_b),
        in_specs=[zspec(0), zspec(1), zspec(2), lspec,
                  pl.BlockSpec((1, 3 * d), fixed), pl.BlockSpec((1, LORA_PAD), fixed),
                  vec, pl.BlockSpec((DECAY_LORA, d), fixed),
                  vec, pl.BlockSpec((AAA_LORA, d), fixed),
                  pl.BlockSpec((GATE_LORA, d), fixed), vec, vec, vec,
                  pl.BlockSpec((CHUNK, CHUNK), fixed), pl.BlockSpec((d, LANES), fixed),
                  pl.BlockSpec((LANES, d), fixed)],
        out_specs=[pl.BlockSpec((tm, d), row)] * 7 +
                  [pl.BlockSpec((tm // CHUNK, 1, d), lambda b, t: (b * per_b + t, 0, 0))],
        out_shape=[big] * 7 + [jax.ShapeDtypeStruct((m // CHUNK, 1, d), F32)],
        scratch_shapes=[pltpu.VMEM((1, d), F32)] * 3 + [pltpu.VMEM((1, LORA_PAD), F32)],
        compiler_params=_params(("parallel", "arbitrary")),
    )(zall, zall, zall, zall, mu, mu_lo, w0, w2, a0, a2, g2, k_k, k_a, r_k, tri, hsum, hexp)


def _rwkv_chunk_kernel(xa_ref, xr_ref, zk_ref, zb_ref, v_ref, bonus_ref, g_ref, glast_ref, lnw_ref, lnb_ref,
                       o_ref, state_ref):
    C = CHUNK
    N = HEAD_DIM

    @pl.when(pl.program_id(1) == 0)
    def _():
        state_ref[...] = jnp.zeros_like(state_ref)

    P2 = 2 * N
    lane = lambda shape: lax.broadcasted_iota(jnp.int32, shape, 1)
    row = lambda shape: lax.broadcasted_iota(jnp.int32, shape, 0)
    lo_c = lane((C, P2)) < N
    bd = (row((P2, P2)) < N) == (lane((P2, P2)) < N)
    ri = row((2 * C, 4 * C))
    cj = lane((2 * C, 4 * C)) & (C - 1)
    keep = cj < (ri & (C - 1)) + jnp.where(ri >= C, 1, 0)
    eye2 = ((lane((C, P2)) & (N - 1)) == row((C, P2))).astype(F32)
    zero_b = jnp.zeros((), BF16)

    def blockdiag(m):
        return jnp.where(bd, jnp.concatenate([m, m], axis=0), jnp.zeros((), m.dtype))

    pairs = range(RWKV_HEADS // 2)
    each = lambda f: [f(p) for p in pairs]
    tl = lambda p: slice(p * P2, (p + 1) * P2)
    xa, xr, zk, zb, vv = xa_ref[...], xr_ref[...], zk_ref[...], zb_ref[...], v_ref[...]
    glast = glast_ref[0]
    S_all = state_ref[...]
    X = each(lambda p: jnp.concatenate([xa[:, tl(p)], xr[:, tl(p)]], axis=0))
    Z = each(lambda p: jnp.concatenate([zb[:, tl(p)], zk[:, tl(p)]], axis=0))
    Vb = each(lambda p: vv[:, tl(p)])
    Zm = each(lambda p: jnp.concatenate(
        [jnp.where(lo_c, zb[:, tl(p)], zero_b), jnp.where(lo_c, zero_b, zb[:, tl(p)]),
         jnp.where(lo_c, zk[:, tl(p)], zero_b), jnp.where(lo_c, zero_b, zk[:, tl(p)])], axis=0))

    A = each(lambda p: jnp.where(keep, _dot_nt(X[p], Zm[p]), 0.0))
    Lb = each(lambda p: A[p][:C, :P2].astype(BF16))
    Pm = each(lambda p: _dot(Lb[p], blockdiag(Lb[p])))
    T = each(lambda p: eye2 + A[p][:C, :P2])
    span = 2
    while 2 * span < C:
        R = each(lambda p: _dot(jnp.concatenate([Pm[p], T[p]], axis=0).astype(BF16), blockdiag(Pm[p].astype(BF16))))
        Pm = each(lambda p: R[p][:C])
        T = each(lambda p: T[p] + R[p][C:])
        span *= 2
    T = each(lambda p: T[p] + _dot(T[p].astype(BF16), blockdiag(Pm[p].astype(BF16))))

    XS = each(lambda p: _dot_nt(X[p], S_all[p].astype(BF16)))
    AV = each(lambda p: _dot(A[p][:, P2:].astype(BF16), blockdiag(Vb[p])))
    W = each(lambda p: (XS[p][:C] + AV[p][:C]).astype(BF16))
    U = each(lambda p: _dot(T[p].astype(BF16), blockdiag(W[p])))
    Y = each(lambda p: XS[p][C:] + AV[p][C:] + _dot(A[p][C:, :P2].astype(BF16), blockdiag(U[p].astype(BF16))))
    UVt = each(lambda p: jnp.concatenate([U[p], Vb[p].astype(F32)], axis=0).T.astype(BF16))
    state_ref[...] = jnp.stack(each(
        lambda p: jnp.where(bd, S_all[p] + _dot(UVt[p], Z[p]), 0.0) * glast[:, tl(p)]))

    def group_norm(y):
        def head_mean(t):
            s_lo = jnp.sum(jnp.where(lo_c, t, 0.0), axis=-1, keepdims=True)
            s_hi = jnp.sum(jnp.where(lo_c, 0.0, t), axis=-1, keepdims=True)
            return jnp.where(lo_c, s_lo, s_hi) * (1.0 / N)
        dlt = y - head_mean(y)
        return dlt * lax.rsqrt(head_mean(dlt * dlt) + GN_EPS)

    y = jnp.concatenate(each(lambda p: group_norm(Y[p])), axis=-1) * lnw_ref[...] + lnb_ref[...]
    o_ref[...] = ((y + bonus_ref[...].astype(F32)) * g_ref[...].astype(F32)).astype(o_ref.dtype)


def _rwkv_chunk(xa, xr, zk, zb, v, bonus, g, glast, lnx_w, lnx_b, batch, seq):
    m, c = xa.shape
    C, H, N = CHUNK, RWKV_HEADS, HEAD_DIM
    per_b = seq // C
    row = pl.BlockSpec((C, c), lambda b, t: (b * per_b + t, 0))
    vec = pl.BlockSpec((1, c), lambda b, t: (0, 0))
    return pl.pallas_call(
        _rwkv_chunk_kernel,
        grid=(batch, per_b),
        in_specs=[row] * 7 + [pl.BlockSpec((1, 1, c), lambda b, t: (b * per_b + t, 0, 0)), vec, vec],
        out_specs=row,
        out_shape=jax.ShapeDtypeStruct((m, c), BF16),
        scratch_shapes=[pltpu.VMEM((H // 2, 2 * N, 2 * N), F32)],
        compiler_params=_params(("parallel", "arbitrary")),
    )(xa, xr, zk, zb, v, bonus, g, glast, lnx_w, lnx_b)


def _mix_kernel(o1, o2, o3, l1, l2, l3, rw_ref, gate_ref, bg_ref, x_ref, gt_ref, hexp_ref, wa_ref, wr_ref, wo_ref,
                out_ref, os_ref, ls_ref):
    tm = x_ref.shape[0]

    def token_order(o_ref, l_ref):
        d = o_ref.shape[1]
        if d == 1:
            return o_ref[0, 0].astype(F32), l_ref[0, 0]
        rows = tm // d
        nct = os_ref.shape[0]
        for r in range(d):
            o = o_ref[0, r].astype(F32)
            for c in range(nct):
                os_ref[c, pl.ds(r, rows, stride=d), :] = o[:, c * LANES:(c + 1) * LANES]
            ls_ref[pl.ds(r, rows, stride=d), :] = l_ref[0, r]
        return jnp.concatenate([os_ref[c] for c in range(nct)], axis=-1), ls_ref[...]

    oa, la = token_order(o1, l1)
    ob, lb = token_order(o2, l2)
    oc, lc = token_order(o3, l3)
    mx = jnp.maximum(jnp.maximum(la, lb), lc)
    ea, eb, ec = jnp.exp(la - mx), jnp.exp(lb - mx), jnp.exp(lc - mx)
    inv = 1.0 / (ea + eb + ec)
    hexp = hexp_ref[...]
    att = (_dot3(ea * inv, hexp) * oa + _dot3(eb * inv, hexp) * ob + _dot3(ec * inv, hexp) * oc)
    y_att = _dot(att.astype(BF16), wa_ref[...])
    y_rwkv = _dot(rw_ref[...], wr_ref[...])
    gates = jax.nn.sigmoid(gate_ref[...].astype(F32) + bg_ref[...])
    d = y_att.shape[1]
    mix = gates[:, :d] * y_att + gates[:, d:] * y_rwkv
    out_ref[...] = x_ref[...] + gt_ref[0] * _dot(mix.astype(BF16), wo_ref[...])


def _mix(att_o, att_l, rw, zall, b_gate, x2, gt1, wa, wr, wo, batch, seq, tm):
    m, d = x2.shape
    per_b = seq // tm
    row = lambda w: pl.BlockSpec((tm, w), lambda i: (i, 0))
    full = lambda a: pl.BlockSpec(a.shape, lambda i: (0, 0))

    def dil(arr):
        dd = arr.shape[1]
        return pl.BlockSpec((1, dd, tm // dd, arr.shape[3]), lambda i: (i // per_b, 0, i % per_b, 0))

    hexp = (jnp.arange(LANES)[:, None] == (jnp.arange(ATT_WIDTH) // HEAD_DIM)[None, :]).astype(BF16)
    return pl.pallas_call(
        _mix_kernel,
        grid=(m // tm,),
        in_specs=[dil(a) for a in att_o] + [dil(a) for a in att_l] +
                 [row(d), row(2 * d), full(b_gate), row(d),
                  pl.BlockSpec((1, 1, d), lambda i: (i // per_b, 0, 0)), full(hexp), full(wa), full(wr), full(wo)],
        out_specs=row(d),
        out_shape=jax.ShapeDtypeStruct((m, d), F32),
        scratch_shapes=[pltpu.VMEM((ATT_WIDTH // LANES, tm, LANES), F32), pltpu.VMEM((tm, LANES), F32)],
        compiler_params=_params(("parallel",)),
    )(*att_o, *att_l, rw, zall, b_gate, x2, gt1, hexp, wa, wr, wo)


HALO = 16


def _ffn_kernel(x_ref, xh_ref, nw_ref, sc_ref, sh_ref, gt_ref, wg_ref, wv_ref, cwg_ref, cwv_ref, cbg_ref, cbv_ref,
                wd_ref, nf_ref, o_ref, h_ref, acc_ref, *, per_b):
    i = pl.program_id(0)
    j = pl.program_id(1)
    tm = x_ref.shape[0]

    @pl.when(j == 0)
    def _():
        h_ref[HALO:, :] = _adaln(x_ref[...], nw_ref[...], sc_ref[0], sh_ref[0]).astype(BF16)
        hh = _adaln(xh_ref[...], nw_ref[...], sc_ref[0], sh_ref[0])
        h_ref[:HALO, :] = jnp.where(i % per_b == 0, 0.0, hh).astype(BF16)
        acc_ref[...] = jnp.zeros_like(acc_ref)

    h = h_ref[...]

    def conv(u, cw_ref, cb_ref):
        out = cb_ref[...] + cw_ref[2:3, :] * u[HALO:, :]
        out = out + cw_ref[1:2, :] * u[HALO - 1:HALO - 1 + tm, :]
        return out + cw_ref[0:1, :] * u[HALO - 2:HALO - 2 + tm, :]

    gate = conv(_dot(h, wg_ref[...]), cwg_ref, cbg_ref)
    val = conv(_dot(h, wv_ref[...]), cwv_ref, cbv_ref)
    act = (gate * jax.nn.sigmoid(gate) * val).astype(BF16)
    acc_ref[...] += _dot(act, wd_ref[...])

    @pl.when(j == pl.num_programs(1) - 1)
    def _():
        x2 = x_ref[...] + gt_ref[0] * acc_ref[...]
        y = x2 * lax.rsqrt(jnp.mean(x2 * x2, axis=-1, keepdims=True) + RMS_EPS)
        o_ref[...] = y * nf_ref[...]


def _ffn(x1, nw, sc, sh, gt, w_up, conv_w, conv_b, w_down, nf, seq, tm, tf):
    m, d = x1.shape
    f = w_down.shape[0]
    nj = f // tf
    per_b = seq // tm
    hb = tm // HALO
    rowc = lambda i, j: (i, 0)
    bvec = pl.BlockSpec((1, 1, d), lambda i, j: (i // per_b, 0, 0))
    conv_b2 = conv_b.reshape(1, 2 * f)
    return pl.pallas_call(
        functools.partial(_ffn_kernel, per_b=per_b),
        grid=(m // tm, nj),
        in_specs=[pl.BlockSpec((tm, d), rowc),
                  pl.BlockSpec((HALO, d), lambda i, j: (jnp.maximum(i * hb - 1, 0), 0)),
                  pl.BlockSpec((1, d), lambda i, j: (0, 0)), bvec, bvec, bvec,
                  pl.BlockSpec((d, tf), lambda i, j: (0, j)),
                  pl.BlockSpec((d, tf), lambda i, j: (0, nj + j)),
                  pl.BlockSpec((CONV_WIDTH, tf), lambda i, j: (0, j)),
                  pl.BlockSpec((CONV_WIDTH, tf), lambda i, j: (0, nj + j)),
                  pl.BlockSpec((1, tf), lambda i, j: (0, j)),
                  pl.BlockSpec((1, tf), lambda i, j: (0, nj + j)),
                  pl.BlockSpec((tf, d), lambda i, j: (j, 0)),
                  pl.BlockSpec((1, d), lambda i, j: (0, 0))],
        out_specs=pl.BlockSpec((tm, d), rowc),
        out_shape=jax.ShapeDtypeStruct((m, d), F32),
        scratch_shapes=[pltpu.VMEM((tm + HALO, d), BF16), pltpu.VMEM((tm, d), F32)],
        compiler_params=_params(("parallel", "arbitrary")),
    )(x1, x1, nw, sc, sh, gt, w_up, w_up, conv_w, conv_w, conv_b2, conv_b2, w_down, nf)


def kernel(x, c, w_ada, b_ada, norm1_w, w_in, b_gate, mu_shift, w0, w2, a0, a2, g2, k_k, k_a, r_k, lnx_w, lnx_b,
           w_att_out, w_rwkv_out, w_o, norm2_w, w_up, conv_w, conv_b, w_down, norm_f_w):
    batch, seq, d = x.shape
    assert w_ada.shape[0] == 1, "the fused ffn kernel applies the final RMSNorm; one layer only"
    l = 0
    grp = 3 * ATT_WIDTH
    att_in = len(ATT_PATTERNS) * grp
    lora = DECAY_LORA + AAA_LORA + GATE_LORA
    xf = x.reshape(batch * seq, d)

    ada = _ada(c, w_ada[l], b_ada[l])
    sh1, sc1, gt1, sh2, sc2, gt2 = [t.reshape(batch, 1, d) for t in jnp.split(ada, 6, axis=-1)]
    h = _norm(xf, norm1_w[l].reshape(1, d), sc1, sh1, seq, 1024)

    win = w_in[l]
    w_rest = jnp.concatenate(
        [win[:, att_in + 3 * d + lora:], win[:, att_in:att_in + 3 * d + lora],
         jnp.zeros((d, LORA_PAD - lora), F32)], axis=1).astype(BF16)
    zall = _mm(h, w_rest, BF16, 2048, 512)

    att_o, att_l = [], []
    for gi, (_, dilation) in enumerate(ATT_PATTERNS):
        wg = win[:, gi * grp:(gi + 1) * grp]
        wg = jnp.concatenate([wg[:, :ATT_WIDTH] * (HEAD_DIM ** -0.5), wg[:, ATT_WIDTH:]], axis=1).astype(BF16)
        qkv = _mm_dilate(h, wg, dilation, batch, seq, 2048, ATT_WIDTH)
        o, lse = _attention(qkv.reshape(batch * seq, grp), seq // dilation // ATT_BLOCK)
        att_o.append(o.reshape(batch, dilation, seq // dilation, ATT_WIDTH))
        att_l.append(lse.reshape(batch, dilation, seq // dilation, LANES))

    mu = mu_shift[l]
    mu_lo = jnp.pad(mu[3 * d:], (0, LORA_PAD - lora)).reshape(1, LORA_PAD)
    vec = lambda t: t.reshape(1, d)
    xa, xr, zk, zb, v, bonus, g_out, glast = _rwkv_prep(
        zall, 2 * d, mu[:3 * d].reshape(1, 3 * d), mu_lo, vec(w0[l]), w2[l].astype(BF16), vec(a0[l]),
        a2[l].astype(BF16), g2[l].astype(BF16), vec(k_k[l]), vec(k_a[l]), vec(r_k[l]), batch, seq)
    rw = _rwkv_chunk(xa, xr, zk, zb, v, bonus, g_out, glast, vec(lnx_w[l]), vec(lnx_b[l]), batch, seq)

    x1 = _mix(att_o, att_l, rw, zall, b_gate[l].reshape(1, 2 * d), xf, gt1,
              w_att_out[l].astype(BF16), w_rwkv_out[l].astype(BF16), w_o[l].astype(BF16), batch, seq, 256)
    out = _ffn(x1, norm2_w[l].reshape(1, d), sc2, sh2, gt2, w_up[l].astype(BF16), conv_w[l], conv_b[l],
               w_down[l].astype(BF16), norm_f_w.reshape(1, d), seq, 512, 1408)
    return out.reshape(batch, seq, d)
```

```python
import functools
import math

import jax
import jax.numpy as jnp
from jax import lax
from jax.experimental import pallas as pl
from jax.experimental.pallas import tpu as pltpu

F32 = jnp.float32
BF16 = jnp.bfloat16

LANES = 128
ATT_PATTERNS = ((128, 1), (512, 4), (2048, 16))
ATT_HEADS = 8
HEAD_DIM = 64
ATT_WIDTH = ATT_HEADS * HEAD_DIM
ATT_BLOCK = 128
ATT_TQ = 512
RWKV_HEADS = 16
DECAY_LORA = 64
AAA_LORA = 64
GATE_LORA = 160
PROJ_TN = 512
SHIFT_SLABS = 4
LORA_PAD = PROJ_TN
CONV_WIDTH = 3
RMS_EPS = 1e-6
GN_EPS = 64e-5
DECAY_SCALE = math.exp(-0.5)
CHUNK = 64
CHUNKS_PER_STEP = 4
PREP_ROWS = 256
VMEM_LIMIT = 48 * 1024 * 1024


def _dot(a, b):
    return jnp.dot(a, b, preferred_element_type=F32)


def _dot_nt(a, b):
    return lax.dot_general(a, b, (((1,), (1,)), ((), ())), preferred_element_type=F32)


def _split3(x):
    hi = x.astype(BF16)
    r1 = x - hi.astype(F32)
    mid = r1.astype(BF16)
    lo = (r1 - mid.astype(F32)).astype(BF16)
    return hi, mid, lo


def _dot3(x, w):
    hi, mid, lo = _split3(x)
    return _dot(hi, w) + _dot(mid, w) + _dot(lo, w)


def _dot2(x, w):
    hi = x.astype(BF16)
    lo = (x - hi.astype(F32)).astype(BF16)
    return _dot(hi, w) + _dot(lo, w)


def _params(sem):
    return pltpu.CompilerParams(dimension_semantics=sem, vmem_limit_bytes=VMEM_LIMIT)


def _ada_kernel(c_ref, w_ref, b_ref, o_ref):
    ch, cl, _ = _split3(c_ref[...])
    wh, wl, _ = _split3(w_ref[...])
    o_ref[...] = _dot(ch, wh) + _dot(cl, wh) + _dot(ch, wl) + b_ref[...]


def _ada(c, w_ada, b_ada):
    b, d = c.shape
    n = w_ada.shape[1]
    tn = 1536
    return pl.pallas_call(
        _ada_kernel,
        grid=(n // tn,),
        in_specs=[pl.BlockSpec((b, d), lambda j: (0, 0)),
                  pl.BlockSpec((d, tn), lambda j: (0, j)),
                  pl.BlockSpec((1, tn), lambda j: (0, j))],
        out_specs=pl.BlockSpec((b, tn), lambda j: (0, j)),
        out_shape=jax.ShapeDtypeStruct((b, n), F32),
        compiler_params=_params(("arbitrary",)),
    )(c, w_ada, b_ada.reshape(1, n))


def _adaln(x, nw, sc, sh):
    y = x * lax.rsqrt(jnp.mean(x * x, axis=-1, keepdims=True) + RMS_EPS)
    return (y * nw) * (1.0 + sc) + sh


def _norm_kernel(x_ref, nw_ref, sc_ref, sh_ref, o_ref):
    o_ref[...] = _adaln(x_ref[...], nw_ref[...], sc_ref[0], sh_ref[0]).astype(o_ref.dtype)


def _norm(x2, nw, sc, sh, seq, tm):
    m, d = x2.shape
    per_b = seq // tm
    bvec = pl.BlockSpec((1, 1, d), lambda i: (i // per_b, 0, 0))
    return pl.pallas_call(
        _norm_kernel,
        grid=(m // tm,),
        in_specs=[pl.BlockSpec((tm, d), lambda i: (i, 0)), pl.BlockSpec((1, d), lambda i: (0, 0)), bvec, bvec],
        out_specs=pl.BlockSpec((tm, d), lambda i: (i, 0)),
        out_shape=jax.ShapeDtypeStruct((m, d), BF16),
        compiler_params=_params(("parallel",)),
    )(x2, nw, sc, sh)


def _mm_shift_kernel(a_ref, w_ref, mu_ref, o_ref, carry_ref, *, per_b):
    i = pl.program_id(0)
    j = pl.program_id(1)
    tm = a_ref.shape[0]

    @pl.when((i == 0) & (j == 0))
    def _():
        carry_ref[...] = jnp.zeros_like(carry_ref)

    before = jnp.where(i % per_b == 0, 0.0, carry_ref[j])
    w = w_ref[...]
    mu = mu_ref[...]
    slab = tm // SHIFT_SLABS
    first8 = lax.broadcasted_iota(jnp.int32, (8, w.shape[1]), 0) == 0
    outs = []
    for s in range(SHIFT_SLABS):
        z = _dot(a_ref[s * slab:(s + 1) * slab, :], w)
        rolled = pltpu.roll(z, 1, axis=0)
        prev = jnp.concatenate([jnp.where(first8, before, rolled[:8]), rolled[8:]], axis=0)
        outs.append((z + (prev - z) * mu).astype(o_ref.dtype))
        before = z[slab - 1:slab, :]
    carry_ref[j] = before
    o_ref[...] = jnp.concatenate(outs, axis=0)


def _mm_shift(a, w, mu, out_dtype, seq, tm, tn):
    m, k = a.shape
    n = w.shape[1]
    return pl.pallas_call(
        functools.partial(_mm_shift_kernel, per_b=seq // tm),
        grid=(m // tm, n // tn),
        in_specs=[pl.BlockSpec((tm, k), lambda i, j: (i, 0)), pl.BlockSpec((k, tn), lambda i, j: (0, j)),
                  pl.BlockSpec((1, tn), lambda i, j: (0, j))],
        out_specs=pl.BlockSpec((tm, tn), lambda i, j: (i, j)),
        out_shape=jax.ShapeDtypeStruct((m, n), out_dtype),
        scratch_shapes=[pltpu.VMEM((n // tn, 1, tn), F32)],
        compiler_params=_params(("arbitrary", "arbitrary")),
    )(a, w, mu)


MAX_ROW_STRIDE = 4


def _mm_dilate_kernel(a_ref, w_ref, o_ref, acc_ref, tmp_ref, *, dilation):
    tm = a_ref.shape[0]
    rows = tm // dilation
    if dilation == 1:
        o_ref[0, 0] = _dot(a_ref[...], w_ref[...]).astype(o_ref.dtype)
        return
    acc = _dot(a_ref[...], w_ref[...])
    nct = acc.shape[1] // LANES
    for c in range(nct):
        acc_ref[c] = acc[:, c * LANES:(c + 1) * LANES]

    def emit(r, src_ref, start, stride):
        o_ref[0, r] = jnp.concatenate(
            [src_ref[c, pl.ds(start, rows, stride=stride), :] for c in range(nct)], axis=-1).astype(o_ref.dtype)

    if dilation <= MAX_ROW_STRIDE:
        for r in range(dilation):
            emit(r, acc_ref, r, dilation)
        return
    s1 = MAX_ROW_STRIDE
    s2 = dilation // s1
    part = tm // s1
    for q in range(s1):
        for c in range(nct):
            tmp_ref[c, q * part:(q + 1) * part, :] = acc_ref[c, pl.ds(q, part, stride=s1), :]
    for q in range(s1):
        for q2 in range(s2):
            emit(q + s1 * q2, tmp_ref, q * part + q2, s2)


def _mm_dilate(a, w, dilation, batch, seq, tm, tn):
    m, k = a.shape
    n = w.shape[1]
    per_b = seq // tm
    rows = tm // dilation
    return pl.pallas_call(
        functools.partial(_mm_dilate_kernel, dilation=dilation),
        grid=(m // tm, n // tn),
        in_specs=[pl.BlockSpec((tm, k), lambda i, j: (i, 0)), pl.BlockSpec((k, tn), lambda i, j: (0, j))],
        out_specs=pl.BlockSpec((1, dilation, rows, tn), lambda i, j: (i // per_b, 0, i % per_b, j)),
        out_shape=jax.ShapeDtypeStruct((batch, dilation, seq // dilation, n), BF16),
        scratch_shapes=[pltpu.VMEM((tn // LANES, tm, LANES), F32)] * 2,
        compiler_params=_params(("parallel", "arbitrary")),
    )(a, w)


def _att_kernel(q_ref, k_ref, kh_ref, v_ref, vh_ref, o_ref, l_ref, *, blocks_per_seq):
    i = pl.program_id(0)
    nq = q_ref.shape[0] // ATT_BLOCK
    qi = lax.broadcasted_iota(jnp.int32, (ATT_BLOCK, 2 * ATT_BLOCK), 0)
    kj = lax.broadcasted_iota(jnp.int32, (ATT_BLOCK, 2 * ATT_BLOCK), 1)
    band = (kj >= qi) & (kj <= qi + ATT_BLOCK)
    neg = jnp.float32(-1e30)
    bias_std = jnp.where(band, 0.0, neg)
    bias_first = jnp.where(band & (kj >= ATT_BLOCK), 0.0, neg)
    lane = lax.broadcasted_iota(jnp.int32, (ATT_BLOCK, LANES), 1)
    o_blocks, l_blocks = [], []
    for qb in range(nq):
        rows = slice(qb * ATT_BLOCK, (qb + 1) * ATT_BLOCK)
        prev = slice((qb - 1) * ATT_BLOCK, qb * ATT_BLOCK)
        first = (i * nq + qb) % blocks_per_seq == 0
        bias = jnp.where(first, bias_first, bias_std)
        q = q_ref[rows, :]
        k = jnp.concatenate([kh_ref[...] if qb == 0 else k_ref[prev, :], k_ref[rows, :]], axis=0)
        v = jnp.concatenate([vh_ref[...] if qb == 0 else v_ref[prev, :], v_ref[rows, :]], axis=0)
        outs = []
        lse_tile = jnp.zeros((ATT_BLOCK, LANES), F32)
        for hp in range(ATT_HEADS // 2):
            tile = slice(hp * LANES, (hp + 1) * LANES)
            q2, k2, v2 = q[:, tile], k[:, tile], v[:, tile]
            halves = []
            for side in range(2):
                own = (lane < HEAD_DIM) if side == 0 else (lane >= HEAD_DIM)
                s = _dot_nt(jnp.where(own, q2, jnp.zeros((), q2.dtype)), k2) + bias
                m = jnp.max(s, axis=-1, keepdims=True)
                p = jnp.exp(s - m)
                den = jnp.sum(p, axis=-1, keepdims=True)
                halves.append(_dot(p.astype(BF16), v2) / den)
                lse_tile = jnp.where(lane == 2 * hp + side, m + jnp.log(den), lse_tile)
            outs.append(jnp.where(lane < HEAD_DIM, halves[0], halves[1]))
        o_blocks.append(jnp.concatenate(outs, axis=-1).astype(o_ref.dtype))
        l_blocks.append(lse_tile)
    o_ref[...] = jnp.concatenate(o_blocks, axis=0)
    l_ref[...] = jnp.concatenate(l_blocks, axis=0)


def _attention(qkv, blocks_per_seq):
    rows = qkv.shape[0]
    tq = ATT_TQ
    hb = tq // ATT_BLOCK
    main = lambda c: pl.BlockSpec((tq, ATT_WIDTH), lambda i: (i, c))
    halo = lambda c: pl.BlockSpec((ATT_BLOCK, ATT_WIDTH), lambda i: (jnp.maximum(i * hb - 1, 0), c))
    return pl.pallas_call(
        functools.partial(_att_kernel, blocks_per_seq=blocks_per_seq),
        grid=(rows // tq,),
        in_specs=[main(0), main(1), halo(1), main(2), halo(2)],
        out_specs=[pl.BlockSpec((tq, ATT_WIDTH), lambda i: (i, 0)), pl.BlockSpec((tq, LANES), lambda i: (i, 0))],
        out_shape=[jax.ShapeDtypeStruct((rows, ATT_WIDTH), BF16), jax.ShapeDtypeStruct((rows, LANES), F32)],
        compiler_params=_params(("parallel",)),
    )(qkv, qkv, qkv, qkv, qkv)


def _rwkv_prep_kernel(zr_ref, zk_ref, zv_ref, zl_ref, w0_ref, w2_ref, a0_ref, a2_ref, g2_ref,
                      kk_ref, ka_ref, rk_ref, tri_ref, hsum_ref, hexp_ref,
                      xa_out, xr_out, zk_out, zb_out, v_out, bonus_out, g_out, glast_out):
    tm = zr_ref.shape[0]
    r = zr_ref[...].astype(F32)
    k = zk_ref[...].astype(F32)
    v = zv_ref[...].astype(F32)
    zl = zl_ref[...].astype(F32)
    w_low = zl[:, :DECAY_LORA]
    a_low = zl[:, DECAY_LORA:DECAY_LORA + AAA_LORA]
    g_low = zl[:, DECAY_LORA + AAA_LORA:DECAY_LORA + AAA_LORA + GATE_LORA]

    wpre = w0_ref[...] + _dot(jnp.tanh(w_low).astype(BF16), w2_ref[...])
    lw = jax.nn.sigmoid(wpre) * (-DECAY_SCALE)
    a = jax.nn.sigmoid(a0_ref[...] + _dot(a_low.astype(BF16), a2_ref[...]))
    g = _dot(jax.nn.sigmoid(g_low).astype(BF16), g2_ref[...])
    km = k * (1.0 + (a - 1.0) * ka_ref[...])

    kn = k * kk_ref[...]
    inv = lax.rsqrt(jnp.maximum(_dot2(kn * kn, hsum_ref[...]), 1e-24))
    kn = kn * _dot2(inv, hexp_ref[...])
    rk_sum = _dot((r * km * rk_ref[...]).astype(BF16), hsum_ref[...])
    bonus = _dot(rk_sum.astype(BF16), hexp_ref[...]) * v

    tri = tri_ref[...]
    for cidx in range(tm // CHUNK):
        rows = slice(cidx * CHUNK, (cidx + 1) * CHUNK)
        lwc = lw[rows]
        l1 = lwc.astype(BF16)
        l2 = (lwc - l1.astype(F32)).astype(BF16)
        cum = _dot(tri, l1) + _dot(tri, l2)
        e_pos = jnp.exp(cum)
        e_neg = jnp.exp(-cum)
        knc = kn[rows]
        xa_out[rows, :] = (-knc * jnp.exp(cum - lwc)).astype(BF16)
        xr_out[rows, :] = (r[rows] * e_pos).astype(BF16)
        zk_out[rows, :] = (km[rows] * e_neg).astype(BF16)
        zb_out[rows, :] = (knc * a[rows] * e_neg).astype(BF16)
        glast_out[cidx] = e_pos[CHUNK - 1:CHUNK, :]
    v_out[...] = v.astype(BF16)
    bonus_out[...] = bonus.astype(BF16)
    g_out[...] = g.astype(BF16)


def _rwkv_prep(zall, col0, w0, w2, a0, a2, g2, k_k, k_a, r_k, batch, seq):
    m = zall.shape[0]
    d = w0.shape[1]
    tm = PREP_ROWS
    per_b = seq // tm
    cb = col0 // d
    row = lambda b, t: (b * per_b + t, 0)
    fixed = lambda b, t: (0, 0)
    zspec = lambda c: pl.BlockSpec((tm, d), lambda b, t: (b * per_b + t, cb + c))
    lspec = pl.BlockSpec((tm, LORA_PAD), lambda b, t: (b * per_b + t, (col0 + 3 * d) // LORA_PAD))
    vec = pl.BlockSpec((1, d), fixed)
    tri = (jnp.arange(CHUNK)[:, None] >= jnp.arange(CHUNK)[None, :]).astype(BF16)
    head_of = jnp.arange(d) // HEAD_DIM
    hsum = (head_of[:, None] == jnp.arange(LANES)[None, :]).astype(BF16)
    hexp = hsum.T
    big = jax.ShapeDtypeStruct((m, d), BF16)
    return pl.pallas_call(
        _rwkv_prep_kernel,
        grid=(batch, per_b),
        in_specs=[zspec(0), zspec(1), zspec(2), lspec,
                  vec, pl.BlockSpec((DECAY_LORA, d), fixed),
                  vec, pl.BlockSpec((AAA_LORA, d), fixed),
                  pl.BlockSpec((GATE_LORA, d), fixed), vec, vec, vec,
                  pl.BlockSpec((CHUNK, CHUNK), fixed), pl.BlockSpec((d, LANES), fixed),
                  pl.BlockSpec((LANES, d), fixed)],
        out_specs=[pl.BlockSpec((tm, d), row)] * 7 +
                  [pl.BlockSpec((tm // CHUNK, 1, d), lambda b, t: (b * per_b + t, 0, 0))],
        out_shape=[big] * 7 + [jax.ShapeDtypeStruct((m // CHUNK, 1, d), F32)],
        compiler_params=_params(("parallel", "parallel")),
    )(zall, zall, zall, zall, w0, w2, a0, a2, g2, k_k, k_a, r_k, tri, hsum, hexp)


def _rwkv_chunk_kernel(xa_ref, xr_ref, zk_ref, zb_ref, v_ref, bonus_ref, g_ref, glast_ref, lnw_ref, lnb_ref,
                       o_ref, state_ref):
    C = CHUNK
    N = HEAD_DIM

    @pl.when(pl.program_id(1) == 0)
    def _():
        state_ref[...] = jnp.zeros_like(state_ref)

    P2 = 2 * N
    lane = lambda shape: lax.broadcasted_iota(jnp.int32, shape, 1)
    row = lambda shape: lax.broadcasted_iota(jnp.int32, shape, 0)
    lo_c = lane((C, P2)) < N
    bd = (row((P2, P2)) < N) == (lane((P2, P2)) < N)
    ri = row((2 * C, 4 * C))
    cj = lane((2 * C, 4 * C)) & (C - 1)
    keep = cj < (ri & (C - 1)) + jnp.where(ri >= C, 1, 0)
    eye2 = ((lane((C, P2)) & (N - 1)) == row((C, P2))).astype(F32)
    zero_b = jnp.zeros((), BF16)

    def blockdiag(m):
        return jnp.where(bd, jnp.concatenate([m, m], axis=0), jnp.zeros((), m.dtype))

    pairs = range(RWKV_HEADS // 2)
    each = lambda f: [f(p) for p in pairs]
    tl = lambda p: slice(p * P2, (p + 1) * P2)
    def state_free(rows):
        xa, xr, zk, zb, vv = xa_ref[rows, :], xr_ref[rows, :], zk_ref[rows, :], zb_ref[rows, :], v_ref[rows, :]
        X = each(lambda p: jnp.concatenate([xa[:, tl(p)], xr[:, tl(p)]], axis=0))
        Z = each(lambda p: jnp.concatenate([zb[:, tl(p)], zk[:, tl(p)]], axis=0))
        Vb = each(lambda p: vv[:, tl(p)])
        Zm = each(lambda p: jnp.concatenate(
            [jnp.where(lo_c, zb[:, tl(p)], zero_b), jnp.where(lo_c, zero_b, zb[:, tl(p)]),
             jnp.where(lo_c, zk[:, tl(p)], zero_b), jnp.where(lo_c, zero_b, zk[:, tl(p)])], axis=0))
        A = each(lambda p: jnp.where(keep, _dot_nt(X[p], Zm[p]), 0.0))
        Lb = each(lambda p: A[p][:C, :P2].astype(BF16))
        Pm = each(lambda p: _dot(Lb[p], blockdiag(Lb[p])))
        T = each(lambda p: eye2 + A[p][:C, :P2])
        span = 2
        while 2 * span < C:
            R = each(lambda p: _dot(jnp.concatenate([Pm[p], T[p]], axis=0).astype(BF16),
                                    blockdiag(Pm[p].astype(BF16))))
            Pm = each(lambda p: R[p][:C])
            T = each(lambda p: T[p] + R[p][C:])
            span *= 2
        Tb = each(lambda p: (T[p] + _dot(T[p].astype(BF16), blockdiag(Pm[p].astype(BF16)))).astype(BF16))
        AV = each(lambda p: _dot(A[p][:, P2:].astype(BF16), blockdiag(Vb[p])))
        Arb = each(lambda p: A[p][C:, :P2].astype(BF16))
        return X, Z, Vb, Tb, AV, Arb

    def advance(pre, S_all, gl):
        X, Z, Vb, Tb, AV, Arb = pre
        XS = each(lambda p: _dot_nt(X[p], S_all[p].astype(BF16)))
        W = each(lambda p: (XS[p][:C] + AV[p][:C]).astype(BF16))
        U = each(lambda p: _dot(Tb[p], blockdiag(W[p])))
        Y = each(lambda p: XS[p][C:] + AV[p][C:] + _dot(Arb[p], blockdiag(U[p].astype(BF16))))
        UVt = each(lambda p: jnp.concatenate([U[p], Vb[p].astype(F32)], axis=0).T.astype(BF16))
        S_new = each(lambda p: jnp.where(bd, S_all[p] + _dot(UVt[p], Z[p]), 0.0) * gl[:, tl(p)])
        return Y, S_new

    def group_norm(y):
        def head_mean(t):
            s_lo = jnp.sum(jnp.where(lo_c, t, 0.0), axis=-1, keepdims=True)
            s_hi = jnp.sum(jnp.where(lo_c, 0.0, t), axis=-1, keepdims=True)
            return jnp.where(lo_c, s_lo, s_hi) * (1.0 / N)
        dlt = y - head_mean(y)
        return dlt * lax.rsqrt(head_mean(dlt * dlt) + GN_EPS)

    n_chunks = xa_ref.shape[0] // C
    chunk_rows = [slice(ci * C, (ci + 1) * C) for ci in range(n_chunks)]
    pres = [state_free(rows) for rows in chunk_rows]
    S_all = [state_ref[p] for p in pairs]
    ys = []
    for ci in range(n_chunks):
        Y, S_all = advance(pres[ci], S_all, glast_ref[ci])
        ys.append(jnp.concatenate(each(lambda p: group_norm(Y[p])), axis=-1))
    state_ref[...] = jnp.stack(S_all)
    y = jnp.concatenate(ys, axis=0) * lnw_ref[...] + lnb_ref[...]
    o_ref[...] = ((y + bonus_ref[...].astype(F32)) * g_ref[...].astype(F32)).astype(o_ref.dtype)


def _rwkv_chunk(xa, xr, zk, zb, v, bonus, g, glast, lnx_w, lnx_b, batch, seq):
    m, c = xa.shape
    C, H, N = CHUNK, RWKV_HEADS, HEAD_DIM
    nc = CHUNKS_PER_STEP
    per_b = seq // (C * nc)
    row = pl.BlockSpec((C * nc, c), lambda b, t: (b * per_b + t, 0))
    vec = pl.BlockSpec((1, c), lambda b, t: (0, 0))
    return pl.pallas_call(
        _rwkv_chunk_kernel,
        grid=(batch, per_b),
        in_specs=[row] * 7 + [pl.BlockSpec((nc, 1, c), lambda b, t: (b * per_b + t, 0, 0)), vec, vec],
        out_specs=row,
        out_shape=jax.ShapeDtypeStruct((m, c), BF16),
        scratch_shapes=[pltpu.VMEM((H // 2, 2 * N, 2 * N), F32)],
        compiler_params=_params(("parallel", "arbitrary")),
    )(xa, xr, zk, zb, v, bonus, g, glast, lnx_w, lnx_b)


def _mix_kernel(o1, o2, o3, l1, l2, l3, rw_ref, gate_ref, bg_ref, x_ref, gt_ref, hexp_ref, wa_ref, wr_ref, wo_ref,
                out_ref, os_ref, ls_ref):
    tm = x_ref.shape[0]

    def token_order(o_ref, l_ref):
        d = o_ref.shape[1]
        if d == 1:
            return o_ref[0, 0].astype(F32), l_ref[0, 0]
        rows = tm // d
        nct = os_ref.shape[0]
        for r in range(d):
            o = o_ref[0, r].astype(F32)
            for c in range(nct):
                os_ref[c, pl.ds(r, rows, stride=d), :] = o[:, c * LANES:(c + 1) * LANES]
            ls_ref[pl.ds(r, rows, stride=d), :] = l_ref[0, r]
        return jnp.concatenate([os_ref[c] for c in range(nct)], axis=-1), ls_ref[...]

    oa, la = token_order(o1, l1)
    ob, lb = token_order(o2, l2)
    oc, lc = token_order(o3, l3)
    mx = jnp.maximum(jnp.maximum(la, lb), lc)
    ea, eb, ec = jnp.exp(la - mx), jnp.exp(lb - mx), jnp.exp(lc - mx)
    inv = 1.0 / (ea + eb + ec)
    hexp = hexp_ref[...]
    att = (_dot3(ea * inv, hexp) * oa + _dot3(eb * inv, hexp) * ob + _dot3(ec * inv, hexp) * oc)
    y_att = _dot(att.astype(BF16), wa_ref[...])
    y_rwkv = _dot(rw_ref[...], wr_ref[...])
    gates = jax.nn.sigmoid(gate_ref[...].astype(F32) + bg_ref[...])
    d = y_att.shape[1]
    mix = gates[:, :d] * y_att + gates[:, d:] * y_rwkv
    out_ref[...] = x_ref[...] + gt_ref[0] * _dot(mix.astype(BF16), wo_ref[...])


def _mix(att_o, att_l, rw, zall, b_gate, x2, gt1, wa, wr, wo, batch, seq, tm):
    m, d = x2.shape
    per_b = seq // tm
    row = lambda w: pl.BlockSpec((tm, w), lambda i: (i, 0))
    full = lambda a: pl.BlockSpec(a.shape, lambda i: (0, 0))

    def dil(arr):
        dd = arr.shape[1]
        return pl.BlockSpec((1, dd, tm // dd, arr.shape[3]), lambda i: (i // per_b, 0, i % per_b, 0))

    hexp = (jnp.arange(LANES)[:, None] == (jnp.arange(ATT_WIDTH) // HEAD_DIM)[None, :]).astype(BF16)
    return pl.pallas_call(
        _mix_kernel,
        grid=(m // tm,),
        in_specs=[dil(a) for a in att_o] + [dil(a) for a in att_l] +
                 [row(d), row(2 * d), full(b_gate), row(d),
                  pl.BlockSpec((1, 1, d), lambda i: (i // per_b, 0, 0)), full(hexp), full(wa), full(wr), full(wo)],
        out_specs=row(d),
        out_shape=jax.ShapeDtypeStruct((m, d), F32),
        scratch_shapes=[pltpu.VMEM((ATT_WIDTH // LANES, tm, LANES), F32), pltpu.VMEM((tm, LANES), F32)],
        compiler_params=_params(("parallel",)),
    )(*att_o, *att_l, rw, zall, b_gate, x2, gt1, hexp, wa, wr, wo)


HALO = 16


def _ffn_kernel(x_ref, xh_ref, nw_ref, sc_ref, sh_ref, gt_ref, wg_ref, wv_ref, cwg_ref, cwv_ref, cbg_ref, cbv_ref,
                wd_ref, nf_ref, o_ref, h_ref, acc_ref, *, per_b):
    i = pl.program_id(0)
    j = pl.program_id(1)
    tm = x_ref.shape[0]

    @pl.when(j == 0)
    def _():
        h_ref[HALO:, :] = _adaln(x_ref[...], nw_ref[...], sc_ref[0], sh_ref[0]).astype(BF16)
        hh = _adaln(xh_ref[...], nw_ref[...], sc_ref[0], sh_ref[0])
        h_ref[:HALO, :] = jnp.where(i % per_b == 0, 0.0, hh).astype(BF16)
        acc_ref[...] = jnp.zeros_like(acc_ref)

    h = h_ref[...]

    def conv(u, cw_ref, cb_ref):
        out = cb_ref[...] + cw_ref[2:3, :] * u[HALO:, :]
        out = out + cw_ref[1:2, :] * u[HALO - 1:HALO - 1 + tm, :]
        return out + cw_ref[0:1, :] * u[HALO - 2:HALO - 2 + tm, :]

    gate = conv(_dot(h, wg_ref[...]), cwg_ref, cbg_ref)
    val = conv(_dot(h, wv_ref[...]), cwv_ref, cbv_ref)
    act = (gate * jax.nn.sigmoid(gate) * val).astype(BF16)
    acc_ref[...] += _dot(act, wd_ref[...])

    @pl.when(j == pl.num_programs(1) - 1)
    def _():
        x2 = x_ref[...] + gt_ref[0] * acc_ref[...]
        y = x2 * lax.rsqrt(jnp.mean(x2 * x2, axis=-1, keepdims=True) + RMS_EPS)
        o_ref[...] = y * nf_ref[...]


def _ffn(x1, nw, sc, sh, gt, w_up, conv_w, conv_b, w_down, nf, seq, tm, tf):
    m, d = x1.shape
    f = w_down.shape[0]
    nj = f // tf
    per_b = seq // tm
    hb = tm // HALO
    rowc = lambda i, j: (i, 0)
    bvec = pl.BlockSpec((1, 1, d), lambda i, j: (i // per_b, 0, 0))
    conv_b2 = conv_b.reshape(1, 2 * f)
    return pl.pallas_call(
        functools.partial(_ffn_kernel, per_b=per_b),
        grid=(m // tm, nj),
        in_specs=[pl.BlockSpec((tm, d), rowc),
                  pl.BlockSpec((HALO, d), lambda i, j: (jnp.maximum(i * hb - 1, 0), 0)),
                  pl.BlockSpec((1, d), lambda i, j: (0, 0)), bvec, bvec, bvec,
                  pl.BlockSpec((d, tf), lambda i, j: (0, j)),
                  pl.BlockSpec((d, tf), lambda i, j: (0, nj + j)),
                  pl.BlockSpec((CONV_WIDTH, tf), lambda i, j: (0, j)),
                  pl.BlockSpec((CONV_WIDTH, tf), lambda i, j: (0, nj + j)),
                  pl.BlockSpec((1, tf), lambda i, j: (0, j)),
                  pl.BlockSpec((1, tf), lambda i, j: (0, nj + j)),
                  pl.BlockSpec((tf, d), lambda i, j: (j, 0)),
                  pl.BlockSpec((1, d), lambda i, j: (0, 0))],
        out_specs=pl.BlockSpec((tm, d), rowc),
        out_shape=jax.ShapeDtypeStruct((m, d), F32),
        scratch_shapes=[pltpu.VMEM((tm + HALO, d), BF16), pltpu.VMEM((tm, d), F32)],
        compiler_params=_params(("parallel", "arbitrary")),
    )(x1, x1, nw, sc, sh, gt, w_up, w_up, conv_w, conv_w, conv_b2, conv_b2, w_down, nf)


def kernel(x, c, w_ada, b_ada, norm1_w, w_in, b_gate, mu_shift, w0, w2, a0, a2, g2, k_k, k_a, r_k, lnx_w, lnx_b,
           w_att_out, w_rwkv_out, w_o, norm2_w, w_up, conv_w, conv_b, w_down, norm_f_w):
    batch, seq, d = x.shape
    assert w_ada.shape[0] == 1, "the fused ffn kernel applies the final RMSNorm; one layer only"
    l = 0
    grp = 3 * ATT_WIDTH
    att_in = len(ATT_PATTERNS) * grp
    lora = DECAY_LORA + AAA_LORA + GATE_LORA
    xf = x.reshape(batch * seq, d)

    ada = _ada(c, w_ada[l], b_ada[l])
    sh1, sc1, gt1, sh2, sc2, gt2 = [t.reshape(batch, 1, d) for t in jnp.split(ada, 6, axis=-1)]
    h = _norm(xf, norm1_w[l].reshape(1, d), sc1, sh1, seq, 1024)

    win = w_in[l]
    w_rest = jnp.concatenate(
        [win[:, att_in + 3 * d + lora:], win[:, att_in:att_in + 3 * d + lora],
         jnp.zeros((d, LORA_PAD - lora), F32)], axis=1).astype(BF16)
    mu_all = jnp.concatenate([jnp.zeros((2 * d,), F32), mu_shift[l], jnp.zeros((LORA_PAD - lora,), F32)])
    zall = _mm_shift(h, w_rest, mu_all.reshape(1, -1), BF16, seq, 2048, PROJ_TN)

    att_o, att_l = [], []
    for gi, (_, dilation) in enumerate(ATT_PATTERNS):
        wg = win[:, gi * grp:(gi + 1) * grp]
        wg = jnp.concatenate([wg[:, :ATT_WIDTH] * (HEAD_DIM ** -0.5), wg[:, ATT_WIDTH:]], axis=1).astype(BF16)
        qkv = _mm_dilate(h, wg, dilation, batch, seq, 2048, ATT_WIDTH)
        o, lse = _attention(qkv.reshape(batch * seq, grp), seq // dilation // ATT_BLOCK)
        att_o.append(o.reshape(batch, dilation, seq // dilation, ATT_WIDTH))
        att_l.append(lse.reshape(batch, dilation, seq // dilation, LANES))

    vec = lambda t: t.reshape(1, d)
    xa, xr, zk, zb, v, bonus, g_out, glast = _rwkv_prep(
        zall, 2 * d, vec(w0[l]), w2[l].astype(BF16), vec(a0[l]),
        a2[l].astype(BF16), g2[l].astype(BF16), vec(k_k[l]), vec(k_a[l]), vec(r_k[l]), batch, seq)
    rw = _rwkv_chunk(xa, xr, zk, zb, v, bonus, g_out, glast, vec(lnx_w[l]), vec(lnx_b[l]), batch, seq)

    x1 = _mix(att_o, att_l, rw, zall, b_gate[l].reshape(1, 2 * d), xf, gt1,
              w_att_out[l].astype(BF16), w_rwkv_out[l].astype(BF16), w_o[l].astype(BF16), batch, seq, 512)
    out = _ffn(x1, norm2_w[l].reshape(1, d), sc2, sh2, gt2, w_up[l].astype(BF16), conv_w[l], conv_b[l],
               w_down[l].astype(BF16), norm_f_w.reshape(1, d), seq, 512, 1408)
    return out.reshape(batch, seq, d)
```

```python
import functools
import math

import jax
import jax.numpy as jnp
from jax import lax
from jax.experimental import pallas as pl
from jax.experimental.pallas import tpu as pltpu

F32 = jnp.float32
BF16 = jnp.bfloat16

LANES = 128
ATT_PATTERNS = ((128, 1), (512, 4), (2048, 16))
ATT_HEADS = 8
HEAD_DIM = 64
ATT_WIDTH = ATT_HEADS * HEAD_DIM
ATT_BLOCK = 128
ATT_TQ = 512
RWKV_HEADS = 16
DECAY_LORA = 64
AAA_LORA = 64
GATE_LORA = 160
PROJ_TN = 512
SHIFT_SLABS = 4
LORA_PAD = PROJ_TN
CONV_WIDTH = 3
RMS_EPS = 1e-6
GN_EPS = 64e-5
DECAY_SCALE = math.exp(-0.5)
CHUNK = 64
CHUNKS_PER_STEP = 4
PREP_ROWS = 256
VMEM_LIMIT = 48 * 1024 * 1024


def _dot(a, b):
    return jnp.dot(a, b, preferred_element_type=F32)


def _dot_nt(a, b):
    return lax.dot_general(a, b, (((1,), (1,)), ((), ())), preferred_element_type=F32)


def _split3(x):
    hi = x.astype(BF16)
    r1 = x - hi.astype(F32)
    mid = r1.astype(BF16)
    lo = (r1 - mid.astype(F32)).astype(BF16)
    return hi, mid, lo


def _dot3(x, w):
    hi, mid, lo = _split3(x)
    return _dot(hi, w) + _dot(mid, w) + _dot(lo, w)


def _dot2(x, w):
    hi = x.astype(BF16)
    lo = (x - hi.astype(F32)).astype(BF16)
    return _dot(hi, w) + _dot(lo, w)


def _params(sem):
    return pltpu.CompilerParams(dimension_semantics=sem, vmem_limit_bytes=VMEM_LIMIT)


def _ada_kernel(c_ref, w_ref, b_ref, o_ref):
    ch, cl, _ = _split3(c_ref[...])
    wh, wl, _ = _split3(w_ref[...])
    o_ref[...] = _dot(ch, wh) + _dot(cl, wh) + _dot(ch, wl) + b_ref[...]


def _ada(c, w_ada, b_ada):
    b, d = c.shape
    n = w_ada.shape[1]
    tn = 1536
    return pl.pallas_call(
        _ada_kernel,
        grid=(n // tn,),
        in_specs=[pl.BlockSpec((b, d), lambda j: (0, 0)),
                  pl.BlockSpec((d, tn), lambda j: (0, j)),
                  pl.BlockSpec((1, tn), lambda j: (0, j))],
        out_specs=pl.BlockSpec((b, tn), lambda j: (0, j)),
        out_shape=jax.ShapeDtypeStruct((b, n), F32),
        compiler_params=_params(("arbitrary",)),
    )(c, w_ada, b_ada.reshape(1, n))


def _adaln(x, nw, sc, sh):
    y = x * lax.rsqrt(jnp.mean(x * x, axis=-1, keepdims=True) + RMS_EPS)
    return (y * nw) * (1.0 + sc) + sh


def _norm_kernel(x_ref, nw_ref, sc_ref, sh_ref, o_ref):
    o_ref[...] = _adaln(x_ref[...], nw_ref[...], sc_ref[0], sh_ref[0]).astype(o_ref.dtype)


def _norm(x2, nw, sc, sh, seq, tm):
    m, d = x2.shape
    per_b = seq // tm
    bvec = pl.BlockSpec((1, 1, d), lambda i: (i // per_b, 0, 0))
    return pl.pallas_call(
        _norm_kernel,
        grid=(m // tm,),
        in_specs=[pl.BlockSpec((tm, d), lambda i: (i, 0)), pl.BlockSpec((1, d), lambda i: (0, 0)), bvec, bvec],
        out_specs=pl.BlockSpec((tm, d), lambda i: (i, 0)),
        out_shape=jax.ShapeDtypeStruct((m, d), BF16),
        compiler_params=_params(("parallel",)),
    )(x2, nw, sc, sh)


def _mm_shift_kernel(a_ref, w_ref, mu_ref, o_ref, carry_ref, *, per_b):
    i = pl.program_id(0)
    j = pl.program_id(1)
    tm = a_ref.shape[0]

    @pl.when((i == 0) & (j == 0))
    def _():
        carry_ref[...] = jnp.zeros_like(carry_ref)

    before = jnp.where(i % per_b == 0, 0.0, carry_ref[j])
    w = w_ref[...]
    mu = mu_ref[...]
    slab = tm // SHIFT_SLABS
    first8 = lax.broadcasted_iota(jnp.int32, (8, w.shape[1]), 0) == 0
    outs = []
    for s in range(SHIFT_SLABS):
        z = _dot(a_ref[s * slab:(s + 1) * slab, :], w)
        rolled = pltpu.roll(z, 1, axis=0)
        prev = jnp.concatenate([jnp.where(first8, before, rolled[:8]), rolled[8:]], axis=0)
        outs.append((z + (prev - z) * mu).astype(o_ref.dtype))
        before = z[slab - 1:slab, :]
    carry_ref[j] = before
    o_ref[...] = jnp.concatenate(outs, axis=0)


def _mm_shift(a, w, mu, out_dtype, seq, tm, tn):
    m, k = a.shape
    n = w.shape[1]
    return pl.pallas_call(
        functools.partial(_mm_shift_kernel, per_b=seq // tm),
        grid=(m // tm, n // tn),
        in_specs=[pl.BlockSpec((tm, k), lambda i, j: (i, 0)), pl.BlockSpec((k, tn), lambda i, j: (0, j)),
                  pl.BlockSpec((1, tn), lambda i, j: (0, j))],
        out_specs=pl.BlockSpec((tm, tn), lambda i, j: (i, j)),
        out_shape=jax.ShapeDtypeStruct((m, n), out_dtype),
        scratch_shapes=[pltpu.VMEM((n // tn, 1, tn), F32)],
        compiler_params=_params(("arbitrary", "arbitrary")),
    )(a, w, mu)


MAX_ROW_STRIDE = 4


def _mm_dilate_kernel(a_ref, w_ref, o_ref, acc_ref, tmp_ref, *, dilation):
    tm = a_ref.shape[0]
    rows = tm // dilation
    if dilation == 1:
        o_ref[0, 0] = _dot(a_ref[...], w_ref[...]).astype(o_ref.dtype)
        return
    acc = _dot(a_ref[...], w_ref[...])
    nct = acc.shape[1] // LANES
    for c in range(nct):
        acc_ref[c] = acc[:, c * LANES:(c + 1) * LANES]

    def emit(r, src_ref, start, stride):
        o_ref[0, r] = jnp.concatenate(
            [src_ref[c, pl.ds(start, rows, stride=stride), :] for c in range(nct)], axis=-1).astype(o_ref.dtype)

    if dilation <= MAX_ROW_STRIDE:
        for r in range(dilation):
            emit(r, acc_ref, r, dilation)
        return
    s1 = MAX_ROW_STRIDE
    s2 = dilation // s1
    part = tm // s1
    for q in range(s1):
        for c in range(nct):
            tmp_ref[c, q * part:(q + 1) * part, :] = acc_ref[c, pl.ds(q, part, stride=s1), :]
    for q in range(s1):
        for q2 in range(s2):
            emit(q + s1 * q2, tmp_ref, q * part + q2, s2)


def _mm_dilate(a, w, dilation, batch, seq, tm, tn):
    m, k = a.shape
    n = w.shape[1]
    per_b = seq // tm
    rows = tm // dilation
    return pl.pallas_call(
        functools.partial(_mm_dilate_kernel, dilation=dilation),
        grid=(m // tm, n // tn),
        in_specs=[pl.BlockSpec((tm, k), lambda i, j: (i, 0)), pl.BlockSpec((k, tn), lambda i, j: (0, j))],
        out_specs=pl.BlockSpec((1, dilation, rows, tn), lambda i, j: (i // per_b, 0, i % per_b, j)),
        out_shape=jax.ShapeDtypeStruct((batch, dilation, seq // dilation, n), BF16),
        scratch_shapes=[pltpu.VMEM((tn // LANES, tm, LANES), F32)] * 2,
        compiler_params=_params(("parallel", "arbitrary")),
    )(a, w)


def _att_kernel(q_ref, k_ref, kh_ref, v_ref, vh_ref, o_ref, l_ref, *, blocks_per_seq):
    i = pl.program_id(0)
    nq = q_ref.shape[0] // ATT_BLOCK
    qi = lax.broadcasted_iota(jnp.int32, (ATT_BLOCK, 2 * ATT_BLOCK), 0)
    kj = lax.broadcasted_iota(jnp.int32, (ATT_BLOCK, 2 * ATT_BLOCK), 1)
    band = (kj >= qi) & (kj <= qi + ATT_BLOCK)
    neg = jnp.float32(-1e30)
    bias_std = jnp.where(band, 0.0, neg)
    bias_first = jnp.where(band & (kj >= ATT_BLOCK), 0.0, neg)
    lane = lax.broadcasted_iota(jnp.int32, (ATT_BLOCK, LANES), 1)
    o_blocks, l_blocks = [], []
    for qb in range(nq):
        rows = slice(qb * ATT_BLOCK, (qb + 1) * ATT_BLOCK)
        prev = slice((qb - 1) * ATT_BLOCK, qb * ATT_BLOCK)
        first = (i * nq + qb) % blocks_per_seq == 0
        bias = jnp.where(first, bias_first, bias_std)
        q = q_ref[rows, :]
        k = jnp.concatenate([kh_ref[...] if qb == 0 else k_ref[prev, :], k_ref[rows, :]], axis=0)
        v = jnp.concatenate([vh_ref[...] if qb == 0 else v_ref[prev, :], v_ref[rows, :]], axis=0)
        outs = []
        lse_tile = jnp.zeros((ATT_BLOCK, LANES), F32)
        for hp in range(ATT_HEADS // 2):
            tile = slice(hp * LANES, (hp + 1) * LANES)
            q2, k2, v2 = q[:, tile], k[:, tile], v[:, tile]
            halves = []
            for side in range(2):
                own = (lane < HEAD_DIM) if side == 0 else (lane >= HEAD_DIM)
                s = _dot_nt(jnp.where(own, q2, jnp.zeros((), q2.dtype)), k2) + bias
                m = jnp.max(s, axis=-1, keepdims=True)
                p = jnp.exp(s - m)
                den = jnp.sum(p, axis=-1, keepdims=True)
                halves.append(_dot(p.astype(BF16), v2) / den)
                lse_tile = jnp.where(lane == 2 * hp + side, m + jnp.log(den), lse_tile)
            outs.append(jnp.where(lane < HEAD_DIM, halves[0], halves[1]))
        o_blocks.append(jnp.concatenate(outs, axis=-1).astype(o_ref.dtype))
        l_blocks.append(lse_tile)
    o_ref[...] = jnp.concatenate(o_blocks, axis=0)
    l_ref[...] = jnp.concatenate(l_blocks, axis=0)


def _attention(qkv, blocks_per_seq):
    rows = qkv.shape[0]
    tq = ATT_TQ
    hb = tq // ATT_BLOCK
    main = lambda c: pl.BlockSpec((tq, ATT_WIDTH), lambda i: (i, c))
    halo = lambda c: pl.BlockSpec((ATT_BLOCK, ATT_WIDTH), lambda i: (jnp.maximum(i * hb - 1, 0), c))
    return pl.pallas_call(
        functools.partial(_att_kernel, blocks_per_seq=blocks_per_seq),
        grid=(rows // tq,),
        in_specs=[main(0), main(1), halo(1), main(2), halo(2)],
        out_specs=[pl.BlockSpec((tq, ATT_WIDTH), lambda i: (i, 0)), pl.BlockSpec((tq, LANES), lambda i: (i, 0))],
        out_shape=[jax.ShapeDtypeStruct((rows, ATT_WIDTH), BF16), jax.ShapeDtypeStruct((rows, LANES), F32)],
        compiler_params=_params(("parallel",)),
    )(qkv, qkv, qkv, qkv, qkv)


def _rwkv_prep_kernel(zr_ref, zk_ref, zv_ref, zl_ref, w0_ref, w2_ref, a0_ref, a2_ref, g2_ref,
                      kk_ref, ka_ref, rk_ref, tri_ref, hsum_ref, hexp_ref,
                      xa_out, xr_out, zt_out, v_out, bonus_out, g_out, glast_out):
    tm = zr_ref.shape[0]
    r = zr_ref[...].astype(F32)
    k = zk_ref[...].astype(F32)
    v = zv_ref[...].astype(F32)
    zl = zl_ref[...].astype(F32)
    w_low = zl[:, :DECAY_LORA]
    a_low = zl[:, DECAY_LORA:DECAY_LORA + AAA_LORA]
    g_low = zl[:, DECAY_LORA + AAA_LORA:DECAY_LORA + AAA_LORA + GATE_LORA]

    wpre = w0_ref[...] + _dot(jnp.tanh(w_low).astype(BF16), w2_ref[...])
    lw = jax.nn.sigmoid(wpre) * (-DECAY_SCALE)
    a = jax.nn.sigmoid(a0_ref[...] + _dot(a_low.astype(BF16), a2_ref[...]))
    g = _dot(jax.nn.sigmoid(g_low).astype(BF16), g2_ref[...])
    km = k * (1.0 + (a - 1.0) * ka_ref[...])

    kn = k * kk_ref[...]
    inv = lax.rsqrt(jnp.maximum(_dot2(kn * kn, hsum_ref[...]), 1e-24))
    kn = kn * _dot2(inv, hexp_ref[...])
    rk_sum = _dot((r * km * rk_ref[...]).astype(BF16), hsum_ref[...])
    bonus = _dot(rk_sum.astype(BF16), hexp_ref[...]) * v

    tri = tri_ref[...]
    odd_head = (lax.broadcasted_iota(jnp.int32, (CHUNK, r.shape[1]), 1) & HEAD_DIM) != 0
    for cidx in range(tm // CHUNK):
        rows = slice(cidx * CHUNK, (cidx + 1) * CHUNK)
        lwc = lw[rows]
        l1 = lwc.astype(BF16)
        l2 = (lwc - l1.astype(F32)).astype(BF16)
        cum = _dot(tri, l1) + _dot(tri, l2)
        e_pos = jnp.exp(cum)
        e_neg = jnp.exp(-cum)
        knc = kn[rows]
        xa_out[rows, :] = (-knc * jnp.exp(cum - lwc)).astype(BF16)
        xr_out[rows, :] = (r[rows] * e_pos).astype(BF16)
        zk = km[rows] * e_neg
        zb = knc * a[rows] * e_neg
        zt = jnp.concatenate([jnp.where(odd_head, zk, zb), jnp.where(odd_head, zb, zk)], axis=0)
        zt_out[cidx] = zt.T.astype(BF16)
        glast_out[cidx] = e_pos[CHUNK - 1:CHUNK, :]
    v_out[...] = v.astype(BF16)
    bonus_out[...] = bonus.astype(BF16)
    g_out[...] = g.astype(BF16)


def _rwkv_prep(zall, col0, w0, w2, a0, a2, g2, k_k, k_a, r_k, batch, seq):
    m = zall.shape[0]
    d = w0.shape[1]
    tm = PREP_ROWS
    per_b = seq // tm
    cb = col0 // d
    row = lambda b, t: (b * per_b + t, 0)
    fixed = lambda b, t: (0, 0)
    zspec = lambda c: pl.BlockSpec((tm, d), lambda b, t: (b * per_b + t, cb + c))
    lspec = pl.BlockSpec((tm, LORA_PAD), lambda b, t: (b * per_b + t, (col0 + 3 * d) // LORA_PAD))
    vec = pl.BlockSpec((1, d), fixed)
    tri = (jnp.arange(CHUNK)[:, None] >= jnp.arange(CHUNK)[None, :]).astype(BF16)
    head_of = jnp.arange(d) // HEAD_DIM
    hsum = (head_of[:, None] == jnp.arange(LANES)[None, :]).astype(BF16)
    hexp = hsum.T
    big = jax.ShapeDtypeStruct((m, d), BF16)
    return pl.pallas_call(
        _rwkv_prep_kernel,
        grid=(batch, per_b),
        in_specs=[zspec(0), zspec(1), zspec(2), lspec,
                  vec, pl.BlockSpec((DECAY_LORA, d), fixed),
                  vec, pl.BlockSpec((AAA_LORA, d), fixed),
                  pl.BlockSpec((GATE_LORA, d), fixed), vec, vec, vec,
                  pl.BlockSpec((CHUNK, CHUNK), fixed), pl.BlockSpec((d, LANES), fixed),
                  pl.BlockSpec((LANES, d), fixed)],
        out_specs=[pl.BlockSpec((tm, d), row)] * 2 +
                  [pl.BlockSpec((tm // CHUNK, d, 2 * CHUNK), lambda b, t: (b * per_b + t, 0, 0))] +
                  [pl.BlockSpec((tm, d), row)] * 3 +
                  [pl.BlockSpec((tm // CHUNK, 1, d), lambda b, t: (b * per_b + t, 0, 0))],
        out_shape=[big] * 2 + [jax.ShapeDtypeStruct((m // CHUNK, d, 2 * CHUNK), BF16)] + [big] * 3 +
                  [jax.ShapeDtypeStruct((m // CHUNK, 1, d), F32)],
        compiler_params=_params(("parallel", "parallel")),
    )(zall, zall, zall, zall, w0, w2, a0, a2, g2, k_k, k_a, r_k, tri, hsum, hexp)


def _rwkv_chunk_kernel(xa_ref, xr_ref, zt_ref, v_ref, bonus_ref, g_ref, glast_ref, lnw_ref, lnb_ref,
                       o_ref, state_ref):
    C = CHUNK
    N = HEAD_DIM

    @pl.when(pl.program_id(1) == 0)
    def _():
        state_ref[...] = jnp.zeros_like(state_ref)

    P2 = 2 * N
    lane = lambda shape: lax.broadcasted_iota(jnp.int32, shape, 1)
    row = lambda shape: lax.broadcasted_iota(jnp.int32, shape, 0)
    lo_c = lane((C, P2)) < N
    bd = (row((P2, P2)) < N) == (lane((P2, P2)) < N)
    ri = row((2 * C, 4 * C))
    cj = lane((2 * C, 4 * C)) & (C - 1)
    keep = cj < (ri & (C - 1)) + jnp.where(ri >= C, 1, 0)
    eye2 = ((lane((C, P2)) & (N - 1)) == row((C, P2))).astype(F32)
    zero_b = jnp.zeros((), BF16)

    def blockdiag(m):
        return jnp.where(bd, jnp.concatenate([m, m], axis=0), jnp.zeros((), m.dtype))

    pairs = range(RWKV_HEADS // 2)
    each = lambda f: [f(p) for p in pairs]
    tl = lambda p: slice(p * P2, (p + 1) * P2)
    def state_free(ci):
        rows = slice(ci * C, (ci + 1) * C)
        xa, xr, vv = xa_ref[rows, :], xr_ref[rows, :], v_ref[rows, :]
        X = each(lambda p: jnp.concatenate([xa[:, tl(p)], xr[:, tl(p)]], axis=0))
        ZT = each(lambda p: zt_ref[ci, tl(p), :])
        Vb = each(lambda p: vv[:, tl(p)])
        A = each(lambda p: jnp.where(keep, _dot(X[p], jnp.concatenate(
            [jnp.where(bd, ZT[p], zero_b), jnp.where(bd, zero_b, ZT[p])], axis=1)), 0.0))
        Lb = each(lambda p: A[p][:C, :P2].astype(BF16))
        Pm = each(lambda p: _dot(Lb[p], blockdiag(Lb[p])))
        T = each(lambda p: eye2 + A[p][:C, :P2])
        span = 2
        while 2 * span < C:
            R = each(lambda p: _dot(jnp.concatenate([Pm[p], T[p]], axis=0).astype(BF16),
                                    blockdiag(Pm[p].astype(BF16))))
            Pm = each(lambda p: R[p][:C])
            T = each(lambda p: T[p] + R[p][C:])
            span *= 2
        Tb = each(lambda p: (T[p] + _dot(T[p].astype(BF16), blockdiag(Pm[p].astype(BF16)))).astype(BF16))
        AV = each(lambda p: _dot(A[p][:, P2:].astype(BF16), jnp.where(
            bd, zero_b, jnp.concatenate([Vb[p], Vb[p]], axis=0))))
        Arb = each(lambda p: A[p][C:, :P2].astype(BF16))
        gcol = each(lambda p: jnp.broadcast_to(glast_ref[ci][:, tl(p)], (P2, P2)).T)
        return X, ZT, Vb, Tb, AV, Arb, gcol

    def advance(pre, H_all):
        X, ZT, Vb, Tb, AV, Arb, gcol = pre
        XS = each(lambda p: _dot(X[p], H_all[p].astype(BF16)))
        W = each(lambda p: (XS[p][:C] + AV[p][:C]).astype(BF16))
        U = each(lambda p: _dot(Tb[p], blockdiag(W[p])))
        Ub = each(lambda p: U[p].astype(BF16))
        Y = each(lambda p: XS[p][C:] + AV[p][C:] + _dot(Arb[p], blockdiag(Ub[p])))
        UV = each(lambda p: jnp.concatenate([jnp.where(lo_c, Ub[p], Vb[p]), jnp.where(lo_c, Vb[p], Ub[p])], axis=0))
        H_new = each(lambda p: jnp.where(bd, H_all[p] + _dot(ZT[p], UV[p]), 0.0) * gcol[p])
        return Y, H_new

    def group_norm(y):
        def head_mean(t):
            s_lo = jnp.sum(jnp.where(lo_c, t, 0.0), axis=-1, keepdims=True)
            s_hi = jnp.sum(jnp.where(lo_c, 0.0, t), axis=-1, keepdims=True)
            return jnp.where(lo_c, s_lo, s_hi) * (1.0 / N)
        dlt = y - head_mean(y)
        return dlt * lax.rsqrt(head_mean(dlt * dlt) + GN_EPS)

    n_chunks = xa_ref.shape[0] // C
    pres = [state_free(ci) for ci in range(n_chunks)]
    H_all = [state_ref[p] for p in pairs]
    ys = []
    for ci in range(n_chunks):
        Y, H_all = advance(pres[ci], H_all)
        ys.append(jnp.concatenate(each(lambda p: group_norm(Y[p])), axis=-1))
    state_ref[...] = jnp.stack(H_all)
    y = jnp.concatenate(ys, axis=0) * lnw_ref[...] + lnb_ref[...]
    o_ref[...] = ((y + bonus_ref[...].astype(F32)) * g_ref[...].astype(F32)).astype(o_ref.dtype)


def _rwkv_chunk(xa, xr, zt, v, bonus, g, glast, lnx_w, lnx_b, batch, seq):
    m, c = xa.shape
    C, H, N = CHUNK, RWKV_HEADS, HEAD_DIM
    nc = CHUNKS_PER_STEP
    per_b = seq // (C * nc)
    row = pl.BlockSpec((C * nc, c), lambda b, t: (b * per_b + t, 0))
    vec = pl.BlockSpec((1, c), lambda b, t: (0, 0))
    per_chunk = lambda w, n: pl.BlockSpec((nc, w, n), lambda b, t: (b * per_b + t, 0, 0))
    return pl.pallas_call(
        _rwkv_chunk_kernel,
        grid=(batch, per_b),
        in_specs=[row, row, per_chunk(c, 2 * C), row, row, row, per_chunk(1, c), vec, vec],
        out_specs=row,
        out_shape=jax.ShapeDtypeStruct((m, c), BF16),
        scratch_shapes=[pltpu.VMEM((H // 2, 2 * N, 2 * N), F32)],
        compiler_params=_params(("parallel", "arbitrary")),
    )(xa, xr, zt, v, bonus, g, glast, lnx_w, lnx_b)


def _mix_kernel(o1, o2, o3, l1, l2, l3, rw_ref, gate_ref, bg_ref, x_ref, gt_ref, hexp_ref, wa_ref, wr_ref, wo_ref,
                out_ref, os_ref, ls_ref):
    tm = x_ref.shape[0]

    def token_order(o_ref, l_ref):
        d = o_ref.shape[1]
        if d == 1:
            return o_ref[0, 0].astype(F32), l_ref[0, 0]
        rows = tm // d
        nct = os_ref.shape[0]
        for r in range(d):
            o = o_ref[0, r].astype(F32)
            for c in range(nct):
                os_ref[c, pl.ds(r, rows, stride=d), :] = o[:, c * LANES:(c + 1) * LANES]
            ls_ref[pl.ds(r, rows, stride=d), :] = l_ref[0, r]
        return jnp.concatenate([os_ref[c] for c in range(nct)], axis=-1), ls_ref[...]

    oa, la = token_order(o1, l1)
    ob, lb = token_order(o2, l2)
    oc, lc = token_order(o3, l3)
    mx = jnp.maximum(jnp.maximum(la, lb), lc)
    ea, eb, ec = jnp.exp(la - mx), jnp.exp(lb - mx), jnp.exp(lc - mx)
    inv = 1.0 / (ea + eb + ec)
    hexp = hexp_ref[...]
    att = (_dot3(ea * inv, hexp) * oa + _dot3(eb * inv, hexp) * ob + _dot3(ec * inv, hexp) * oc)
    y_att = _dot(att.astype(BF16), wa_ref[...])
    y_rwkv = _dot(rw_ref[...], wr_ref[...])
    gates = jax.nn.sigmoid(gate_ref[...].astype(F32) + bg_ref[...])
    d = y_att.shape[1]
    mix = gates[:, :d] * y_att + gates[:, d:] * y_rwkv
    out_ref[...] = x_ref[...] + gt_ref[0] * _dot(mix.astype(BF16), wo_ref[...])


def _mix(att_o, att_l, rw, zall, b_gate, x2, gt1, wa, wr, wo, batch, seq, tm):
    m, d = x2.shape
    per_b = seq // tm
    row = lambda w: pl.BlockSpec((tm, w), lambda i: (i, 0))
    full = lambda a: pl.BlockSpec(a.shape, lambda i: (0, 0))

    def dil(arr):
        dd = arr.shape[1]
        return pl.BlockSpec((1, dd, tm // dd, arr.shape[3]), lambda i: (i // per_b, 0, i % per_b, 0))

    hexp = (jnp.arange(LANES)[:, None] == (jnp.arange(ATT_WIDTH) // HEAD_DIM)[None, :]).astype(BF16)
    return pl.pallas_call(
        _mix_kernel,
        grid=(m // tm,),
        in_specs=[dil(a) for a in att_o] + [dil(a) for a in att_l] +
                 [row(d), row(2 * d), full(b_gate), row(d),
                  pl.BlockSpec((1, 1, d), lambda i: (i // per_b, 0, 0)), full(hexp), full(wa), full(wr), full(wo)],
        out_specs=row(d),
        out_shape=jax.ShapeDtypeStruct((m, d), F32),
        scratch_shapes=[pltpu.VMEM((ATT_WIDTH // LANES, tm, LANES), F32), pltpu.VMEM((tm, LANES), F32)],
        compiler_params=_params(("parallel",)),
    )(*att_o, *att_l, rw, zall, b_gate, x2, gt1, hexp, wa, wr, wo)


HALO = 16


def _ffn_kernel(x_ref, xh_ref, nw_ref, sc_ref, sh_ref, gt_ref, wg_ref, wv_ref, cwg_ref, cwv_ref, cbg_ref, cbv_ref,
                wd_ref, nf_ref, o_ref, h_ref, acc_ref, *, per_b):
    i = pl.program_id(0)
    j = pl.program_id(1)
    tm = x_ref.shape[0]

    @pl.when(j == 0)
    def _():
        h_ref[HALO:, :] = _adaln(x_ref[...], nw_ref[...], sc_ref[0], sh_ref[0]).astype(BF16)
        hh = _adaln(xh_ref[...], nw_ref[...], sc_ref[0], sh_ref[0])
        h_ref[:HALO, :] = jnp.where(i % per_b == 0, 0.0, hh).astype(BF16)
        acc_ref[...] = jnp.zeros_like(acc_ref)

    h = h_ref[...]

    def conv(u, cw_ref, cb_ref):
        out = cb_ref[...] + cw_ref[2:3, :] * u[HALO:, :]
        out = out + cw_ref[1:2, :] * u[HALO - 1:HALO - 1 + tm, :]
        return out + cw_ref[0:1, :] * u[HALO - 2:HALO - 2 + tm, :]

    gate = conv(_dot(h, wg_ref[...]), cwg_ref, cbg_ref)
    val = conv(_dot(h, wv_ref[...]), cwv_ref, cbv_ref)
    act = (gate * jax.nn.sigmoid(gate) * val).astype(BF16)
    acc_ref[...] += _dot(act, wd_ref[...])

    @pl.when(j == pl.num_programs(1) - 1)
    def _():
        x2 = x_ref[...] + gt_ref[0] * acc_ref[...]
        y = x2 * lax.rsqrt(jnp.mean(x2 * x2, axis=-1, keepdims=True) + RMS_EPS)
        o_ref[...] = y * nf_ref[...]


def _ffn(x1, nw, sc, sh, gt, w_up, conv_w, conv_b, w_down, nf, seq, tm, tf):
    m, d = x1.shape
    f = w_down.shape[0]
    nj = f // tf
    per_b = seq // tm
    hb = tm // HALO
    rowc = lambda i, j: (i, 0)
    bvec = pl.BlockSpec((1, 1, d), lambda i, j: (i // per_b, 0, 0))
    conv_b2 = conv_b.reshape(1, 2 * f)
    return pl.pallas_call(
        functools.partial(_ffn_kernel, per_b=per_b),
        grid=(m // tm, nj),
        in_specs=[pl.BlockSpec((tm, d), rowc),
                  pl.BlockSpec((HALO, d), lambda i, j: (jnp.maximum(i * hb - 1, 0), 0)),
                  pl.BlockSpec((1, d), lambda i, j: (0, 0)), bvec, bvec, bvec,
                  pl.BlockSpec((d, tf), lambda i, j: (0, j)),
                  pl.BlockSpec((d, tf), lambda i, j: (0, nj + j)),
                  pl.BlockSpec((CONV_WIDTH, tf), lambda i, j: (0, j)),
                  pl.BlockSpec((CONV_WIDTH, tf), lambda i, j: (0, nj + j)),
                  pl.BlockSpec((1, tf), lambda i, j: (0, j)),
                  pl.BlockSpec((1, tf), lambda i, j: (0, nj + j)),
                  pl.BlockSpec((tf, d), lambda i, j: (j, 0)),
                  pl.BlockSpec((1, d), lambda i, j: (0, 0))],
        out_specs=pl.BlockSpec((tm, d), rowc),
        out_shape=jax.ShapeDtypeStruct((m, d), F32),
        scratch_shapes=[pltpu.VMEM((tm + HALO, d), BF16), pltpu.VMEM((tm, d), F32)],
        compiler_params=_params(("parallel", "arbitrary")),
    )(x1, x1, nw, sc, sh, gt, w_up, w_up, conv_w, conv_w, conv_b2, conv_b2, w_down, nf)


def kernel(x, c, w_ada, b_ada, norm1_w, w_in, b_gate, mu_shift, w0, w2, a0, a2, g2, k_k, k_a, r_k, lnx_w, lnx_b,
           w_att_out, w_rwkv_out, w_o, norm2_w, w_up, conv_w, conv_b, w_down, norm_f_w):
    batch, seq, d = x.shape
    assert w_ada.shape[0] == 1, "the fused ffn kernel applies the final RMSNorm; one layer only"
    l = 0
    grp = 3 * ATT_WIDTH
    att_in = len(ATT_PATTERNS) * grp
    lora = DECAY_LORA + AAA_LORA + GATE_LORA
    xf = x.reshape(batch * seq, d)

    ada = _ada(c, w_ada[l], b_ada[l])
    sh1, sc1, gt1, sh2, sc2, gt2 = [t.reshape(batch, 1, d) for t in jnp.split(ada, 6, axis=-1)]
    h = _norm(xf, norm1_w[l].reshape(1, d), sc1, sh1, seq, 1024)

    win = w_in[l]
    w_rest = jnp.concatenate(
        [win[:, att_in + 3 * d + lora:], win[:, att_in:att_in + 3 * d + lora],
         jnp.zeros((d, LORA_PAD - lora), F32)], axis=1).astype(BF16)
    mu_all = jnp.concatenate([jnp.zeros((2 * d,), F32), mu_shift[l], jnp.zeros((LORA_PAD - lora,), F32)])
    zall = _mm_shift(h, w_rest, mu_all.reshape(1, -1), BF16, seq, 2048, PROJ_TN)

    att_o, att_l = [], []
    for gi, (_, dilation) in enumerate(ATT_PATTERNS):
        wg = win[:, gi * grp:(gi + 1) * grp]
        wg = jnp.concatenate([wg[:, :ATT_WIDTH] * (HEAD_DIM ** -0.5), wg[:, ATT_WIDTH:]], axis=1).astype(BF16)
        qkv = _mm_dilate(h, wg, dilation, batch, seq, 2048, ATT_WIDTH)
        o, lse = _attention(qkv.reshape(batch * seq, grp), seq // dilation // ATT_BLOCK)
        att_o.append(o.reshape(batch, dilation, seq // dilation, ATT_WIDTH))
        att_l.append(lse.reshape(batch, dilation, seq // dilation, LANES))

    vec = lambda t: t.reshape(1, d)
    xa, xr, zt, v, bonus, g_out, glast = _rwkv_prep(
        zall, 2 * d, vec(w0[l]), w2[l].astype(BF16), vec(a0[l]),
        a2[l].astype(BF16), g2[l].astype(BF16), vec(k_k[l]), vec(k_a[l]), vec(r_k[l]), batch, seq)
    rw = _rwkv_chunk(xa, xr, zt, v, bonus, g_out, glast, vec(lnx_w[l]), vec(lnx_b[l]), batch, seq)

    x1 = _mix(att_o, att_l, rw, zall, b_gate[l].reshape(1, 2 * d), xf, gt1,
              w_att_out[l].astype(BF16), w_rwkv_out[l].astype(BF16), w_o[l].astype(BF16), batch, seq, 512)
    out = _ffn(x1, norm2_w[l].reshape(1, d), sc2, sh2, gt2, w_up[l].astype(BF16), conv_w[l], conv_b[l],
               w_down[l].astype(BF16), norm_f_w.reshape(1, d), seq, 512, 1408)
    return out.reshape(batch, seq, d)
```

```python
import functools
import math

import jax
import jax.numpy as jnp
from jax import lax
from jax.experimental import pallas as pl
from jax.experimental.pallas import tpu as pltpu

F32 = jnp.float32
BF16 = jnp.bfloat16

LANES = 128
ATT_PATTERNS = ((128, 1), (512, 4), (2048, 16))
ATT_HEADS = 8
HEAD_DIM = 64
ATT_WIDTH = ATT_HEADS * HEAD_DIM
ATT_BLOCK = 128
ATT_TQ = 512
RWKV_HEADS = 16
DECAY_LORA = 64
AAA_LORA = 64
GATE_LORA = 160
PROJ_TN = 512
SHIFT_SLABS = 4
LORA_PAD = PROJ_TN
CONV_WIDTH = 3
RMS_EPS = 1e-6
GN_EPS = 64e-5
DECAY_SCALE = math.exp(-0.5)
CHUNK = 64
CHUNKS_PER_STEP = 4
VMEM_LIMIT = 48 * 1024 * 1024
FFN_VMEM_LIMIT = 56 * 1024 * 1024


def _dot(a, b):
    return jnp.dot(a, b, preferred_element_type=F32)


def _dot_nt(a, b):
    return lax.dot_general(a, b, (((1,), (1,)), ((), ())), preferred_element_type=F32)


def _split3(x):
    hi = x.astype(BF16)
    r1 = x - hi.astype(F32)
    mid = r1.astype(BF16)
    lo = (r1 - mid.astype(F32)).astype(BF16)
    return hi, mid, lo


def _dot3(x, w):
    hi, mid, lo = _split3(x)
    return _dot(hi, w) + _dot(mid, w) + _dot(lo, w)


def _dot2(x, w):
    hi = x.astype(BF16)
    lo = (x - hi.astype(F32)).astype(BF16)
    return _dot(hi, w) + _dot(lo, w)


def _params(sem):
    return pltpu.CompilerParams(dimension_semantics=sem, vmem_limit_bytes=VMEM_LIMIT)


def _ada_kernel(c_ref, w_ref, b_ref, o_ref):
    ch, cl, _ = _split3(c_ref[...])
    wh, wl, _ = _split3(w_ref[...])
    o_ref[...] = _dot(ch, wh) + _dot(cl, wh) + _dot(ch, wl) + b_ref[...]


def _ada(c, w_ada, b_ada):
    b, d = c.shape
    n = w_ada.shape[1]
    tn = 1536
    return pl.pallas_call(
        _ada_kernel,
        grid=(n // tn,),
        in_specs=[pl.BlockSpec((b, d), lambda j: (0, 0)),
                  pl.BlockSpec((d, tn), lambda j: (0, j)),
                  pl.BlockSpec((1, tn), lambda j: (0, j))],
        out_specs=pl.BlockSpec((b, tn), lambda j: (0, j)),
        out_shape=jax.ShapeDtypeStruct((b, n), F32),
        compiler_params=_params(("arbitrary",)),
    )(c, w_ada, b_ada.reshape(1, n))


def _adaln(x, nw, sc, sh):
    y = x * lax.rsqrt(jnp.mean(x * x, axis=-1, keepdims=True) + RMS_EPS)
    return (y * nw) * (1.0 + sc) + sh


def _norm_kernel(x_ref, nw_ref, sc_ref, sh_ref, o_ref):
    o_ref[...] = _adaln(x_ref[...], nw_ref[...], sc_ref[0], sh_ref[0]).astype(o_ref.dtype)


def _norm(x2, nw, sc, sh, seq, tm):
    m, d = x2.shape
    per_b = seq // tm
    bvec = pl.BlockSpec((1, 1, d), lambda i: (i // per_b, 0, 0))
    return pl.pallas_call(
        _norm_kernel,
        grid=(m // tm,),
        in_specs=[pl.BlockSpec((tm, d), lambda i: (i, 0)), pl.BlockSpec((1, d), lambda i: (0, 0)), bvec, bvec],
        out_specs=pl.BlockSpec((tm, d), lambda i: (i, 0)),
        out_shape=jax.ShapeDtypeStruct((m, d), BF16),
        compiler_params=_params(("parallel",)),
    )(x2, nw, sc, sh)


def _mm_shift_kernel(a_ref, w_ref, mu_ref, o_ref, carry_ref, *, per_b):
    i = pl.program_id(0)
    j = pl.program_id(1)
    tm = a_ref.shape[0]

    @pl.when((i == 0) & (j == 0))
    def _():
        carry_ref[...] = jnp.zeros_like(carry_ref)

    before = jnp.where(i % per_b == 0, 0.0, carry_ref[j])
    w = w_ref[...]
    mu = mu_ref[...]
    slab = tm // SHIFT_SLABS
    first8 = lax.broadcasted_iota(jnp.int32, (8, w.shape[1]), 0) == 0
    outs = []
    for s in range(SHIFT_SLABS):
        z = _dot(a_ref[s * slab:(s + 1) * slab, :], w)
        rolled = pltpu.roll(z, 1, axis=0)
        prev = jnp.concatenate([jnp.where(first8, before, rolled[:8]), rolled[8:]], axis=0)
        outs.append((z + (prev - z) * mu).astype(o_ref.dtype))
        before = z[slab - 1:slab, :]
    carry_ref[j] = before
    o_ref[...] = jnp.concatenate(outs, axis=0)


def _mm_shift(a, w, mu, out_dtype, seq, tm, tn):
    m, k = a.shape
    n = w.shape[1]
    return pl.pallas_call(
        functools.partial(_mm_shift_kernel, per_b=seq // tm),
        grid=(m // tm, n // tn),
        in_specs=[pl.BlockSpec((tm, k), lambda i, j: (i, 0)), pl.BlockSpec((k, tn), lambda i, j: (0, j)),
                  pl.BlockSpec((1, tn), lambda i, j: (0, j))],
        out_specs=pl.BlockSpec((tm, tn), lambda i, j: (i, j)),
        out_shape=jax.ShapeDtypeStruct((m, n), out_dtype),
        scratch_shapes=[pltpu.VMEM((n // tn, 1, tn), F32)],
        compiler_params=_params(("arbitrary", "arbitrary")),
    )(a, w, mu)


MAX_ROW_STRIDE = 4


def _mm_dilate_kernel(a_ref, w_ref, o_ref, acc_ref, tmp_ref, *, dilation, first_tile_scale):
    tm = a_ref.shape[0]
    rows = tm // dilation
    acc = _dot(a_ref[...], w_ref[...]) * jnp.where(pl.program_id(1) == 0, first_tile_scale, 1.0)
    if dilation == 1:
        o_ref[0, 0] = acc.astype(o_ref.dtype)
        return
    nct = acc.shape[1] // LANES
    for c in range(nct):
        acc_ref[c] = acc[:, c * LANES:(c + 1) * LANES]

    def emit(r, src_ref, start, stride):
        o_ref[0, r] = jnp.concatenate(
            [src_ref[c, pl.ds(start, rows, stride=stride), :] for c in range(nct)], axis=-1).astype(o_ref.dtype)

    if dilation <= MAX_ROW_STRIDE:
        for r in range(dilation):
            emit(r, acc_ref, r, dilation)
        return
    s1 = MAX_ROW_STRIDE
    s2 = dilation // s1
    part = tm // s1
    for q in range(s1):
        for c in range(nct):
            tmp_ref[c, q * part:(q + 1) * part, :] = acc_ref[c, pl.ds(q, part, stride=s1), :]
    for q in range(s1):
        for q2 in range(s2):
            emit(q + s1 * q2, tmp_ref, q * part + q2, s2)


def _mm_dilate(a, w, col0, n, first_tile_scale, dilation, batch, seq, tm, tn):
    m, k = a.shape
    cb = col0 // tn
    per_b = seq // tm
    rows = tm // dilation
    return pl.pallas_call(
        functools.partial(_mm_dilate_kernel, dilation=dilation, first_tile_scale=first_tile_scale),
        grid=(m // tm, n // tn),
        in_specs=[pl.BlockSpec((tm, k), lambda i, j: (i, 0)), pl.BlockSpec((k, tn), lambda i, j: (0, cb + j))],
        out_specs=pl.BlockSpec((1, dilation, rows, tn), lambda i, j: (i // per_b, 0, i % per_b, j)),
        out_shape=jax.ShapeDtypeStruct((batch, dilation, seq // dilation, n), BF16),
        scratch_shapes=[pltpu.VMEM((tn // LANES, tm, LANES), F32)] * 2,
        compiler_params=_params(("parallel", "arbitrary")),
    )(a, w)


def _att_kernel(q_ref, k_ref, kh_ref, v_ref, vh_ref, o_ref, l_ref, *, blocks_per_seq):
    i = pl.program_id(0)
    nq = q_ref.shape[0] // ATT_BLOCK
    qi = lax.broadcasted_iota(jnp.int32, (ATT_BLOCK, 2 * ATT_BLOCK), 0)
    kj = lax.broadcasted_iota(jnp.int32, (ATT_BLOCK, 2 * ATT_BLOCK), 1)
    band = (kj >= qi) & (kj <= qi + ATT_BLOCK)
    neg = jnp.float32(-1e30)
    bias_std = jnp.where(band, 0.0, neg)
    bias_first = jnp.where(band & (kj >= ATT_BLOCK), 0.0, neg)
    lane = lax.broadcasted_iota(jnp.int32, (ATT_BLOCK, LANES), 1)
    o_blocks, l_blocks = [], []
    for qb in range(nq):
        rows = slice(qb * ATT_BLOCK, (qb + 1) * ATT_BLOCK)
        prev = slice((qb - 1) * ATT_BLOCK, qb * ATT_BLOCK)
        first = (i * nq + qb) % blocks_per_seq == 0
        bias = jnp.where(first, bias_first, bias_std)
        q = q_ref[rows, :]
        k = jnp.concatenate([kh_ref[...] if qb == 0 else k_ref[prev, :], k_ref[rows, :]], axis=0)
        v = jnp.concatenate([vh_ref[...] if qb == 0 else v_ref[prev, :], v_ref[rows, :]], axis=0)
        outs = []
        lse_tile = jnp.zeros((ATT_BLOCK, LANES), F32)
        for hp in range(ATT_HEADS // 2):
            tile = slice(hp * LANES, (hp + 1) * LANES)
            q2, k2, v2 = q[:, tile], k[:, tile], v[:, tile]
            halves = []
            for side in range(2):
                own = (lane < HEAD_DIM) if side == 0 else (lane >= HEAD_DIM)
                s = _dot_nt(jnp.where(own, q2, jnp.zeros((), q2.dtype)), k2) + bias
                m = jnp.max(s, axis=-1, keepdims=True)
                p = jnp.exp(s - m)
                den = jnp.sum(p, axis=-1, keepdims=True)
                halves.append(_dot(p.astype(BF16), v2) / den)
                lse_tile = jnp.where(lane == 2 * hp + side, m + jnp.log(den), lse_tile)
            outs.append(jnp.where(lane < HEAD_DIM, halves[0], halves[1]))
        o_blocks.append(jnp.concatenate(outs, axis=-1).astype(o_ref.dtype))
        l_blocks.append(lse_tile)
    o_ref[...] = jnp.concatenate(o_blocks, axis=0)
    l_ref[...] = jnp.concatenate(l_blocks, axis=0)


def _attention(qkv, blocks_per_seq):
    rows = qkv.shape[0]
    tq = ATT_TQ
    hb = tq // ATT_BLOCK
    main = lambda c: pl.BlockSpec((tq, ATT_WIDTH), lambda i: (i, c))
    halo = lambda c: pl.BlockSpec((ATT_BLOCK, ATT_WIDTH), lambda i: (jnp.maximum(i * hb - 1, 0), c))
    return pl.pallas_call(
        functools.partial(_att_kernel, blocks_per_seq=blocks_per_seq),
        grid=(rows // tq,),
        in_specs=[main(0), main(1), halo(1), main(2), halo(2)],
        out_specs=[pl.BlockSpec((tq, ATT_WIDTH), lambda i: (i, 0)), pl.BlockSpec((tq, LANES), lambda i: (i, 0))],
        out_shape=[jax.ShapeDtypeStruct((rows, ATT_WIDTH), BF16), jax.ShapeDtypeStruct((rows, LANES), F32)],
        compiler_params=_params(("parallel",)),
    )(qkv, qkv, qkv, qkv, qkv)


def _rwkv_kernel(zr_ref, zk_ref, zv_ref, zl_ref, w0_ref, w2_ref, a0_ref, a2_ref, g2_ref, kk_ref, ka_ref, rk_ref,
                 tri_ref, hsum_ref, hexp_ref, lnw_ref, lnb_ref, o_ref, state_ref):
    C = CHUNK
    N = HEAD_DIM

    @pl.when(pl.program_id(1) == 0)
    def _():
        state_ref[...] = jnp.zeros_like(state_ref)

    r = zr_ref[...].astype(F32)
    k = zk_ref[...].astype(F32)
    v = zv_ref[...].astype(F32)
    zl = zl_ref[...].astype(F32)
    w_low = zl[:, :DECAY_LORA]
    a_low = zl[:, DECAY_LORA:DECAY_LORA + AAA_LORA]
    g_low = zl[:, DECAY_LORA + AAA_LORA:DECAY_LORA + AAA_LORA + GATE_LORA]

    wpre = w0_ref[...] + _dot(jnp.tanh(w_low).astype(BF16), w2_ref[...])
    lw = jax.nn.sigmoid(wpre) * (-DECAY_SCALE)
    a = jax.nn.sigmoid(a0_ref[...] + _dot(a_low.astype(BF16), a2_ref[...]))
    g = _dot(jax.nn.sigmoid(g_low).astype(BF16), g2_ref[...])
    km = k * (1.0 + (a - 1.0) * ka_ref[...])

    kn = k * kk_ref[...]
    inv = lax.rsqrt(jnp.maximum(_dot2(kn * kn, hsum_ref[...]), 1e-24))
    kn = kn * _dot2(inv, hexp_ref[...])
    rk_sum = _dot((r * km * rk_ref[...]).astype(BF16), hsum_ref[...])
    bonus = _dot(rk_sum.astype(BF16), hexp_ref[...]) * v
    vb = v.astype(BF16)

    P2 = 2 * N
    lane = lambda shape: lax.broadcasted_iota(jnp.int32, shape, 1)
    row = lambda shape: lax.broadcasted_iota(jnp.int32, shape, 0)
    lo_c = lane((C, P2)) < N
    bd = (row((P2, P2)) < N) == (lane((P2, P2)) < N)
    ri = row((2 * C, 4 * C))
    cj = lane((2 * C, 4 * C)) & (C - 1)
    keep = cj < (ri & (C - 1)) + jnp.where(ri >= C, 1, 0)
    eye2 = ((lane((C, P2)) & (N - 1)) == row((C, P2))).astype(F32)
    zero_b = jnp.zeros((), BF16)
    odd_head = (lane((C, r.shape[1])) & N) != 0
    tri = tri_ref[...]

    def blockdiag(m):
        return jnp.where(bd, jnp.concatenate([m, m], axis=0), jnp.zeros((), m.dtype))

    pairs = range(RWKV_HEADS // 2)
    each = lambda f: [f(p) for p in pairs]
    tl = lambda p: slice(p * P2, (p + 1) * P2)

    def state_free(ci):
        rows = slice(ci * C, (ci + 1) * C)
        lwc = lw[rows]
        l1 = lwc.astype(BF16)
        l2 = (lwc - l1.astype(F32)).astype(BF16)
        cum = _dot(tri, l1) + _dot(tri, l2)
        e_pos = jnp.exp(cum)
        e_neg = jnp.exp(-cum)
        knc = kn[rows]
        xa = (-knc * jnp.exp(cum - lwc)).astype(BF16)
        xr = (r[rows] * e_pos).astype(BF16)
        zk = km[rows] * e_neg
        zb = knc * a[rows] * e_neg
        zt = jnp.concatenate([jnp.where(odd_head, zk, zb), jnp.where(odd_head, zb, zk)], axis=0).T.astype(BF16)
        glast = e_pos[C - 1:C, :]

        X = each(lambda p: jnp.concatenate([xa[:, tl(p)], xr[:, tl(p)]], axis=0))
        ZT = each(lambda p: zt[tl(p), :])
        Vb = each(lambda p: vb[rows, tl(p)])
        A = each(lambda p: jnp.where(keep, _dot(X[p], jnp.concatenate(
            [jnp.where(bd, ZT[p], zero_b), jnp.where(bd, zero_b, ZT[p])], axis=1)), 0.0))
        Lb = each(lambda p: A[p][:C, :P2].astype(BF16))
        Pm = each(lambda p: _dot(Lb[p], blockdiag(Lb[p])))
        T = each(lambda p: eye2 + A[p][:C, :P2])
        span = 2
        while 2 * span < C:
            R = each(lambda p: _dot(jnp.concatenate([Pm[p], T[p]], axis=0).astype(BF16),
                                    blockdiag(Pm[p].astype(BF16))))
            Pm = each(lambda p: R[p][:C])
            T = each(lambda p: T[p] + R[p][C:])
            span *= 2
        Tb = each(lambda p: (T[p] + _dot(T[p].astype(BF16), blockdiag(Pm[p].astype(BF16)))).astype(BF16))
        AV = each(lambda p: _dot(A[p][:, P2:].astype(BF16), jnp.where(
            bd, zero_b, jnp.concatenate([Vb[p], Vb[p]], axis=0))))
        Arb = each(lambda p: A[p][C:, :P2].astype(BF16))
        gcol = each(lambda p: jnp.broadcast_to(glast[:, tl(p)], (P2, P2)).T)
        return X, ZT, Vb, Tb, AV, Arb, gcol

    def advance(pre, H_all):
        X, ZT, Vb, Tb, AV, Arb, gcol = pre
        XS = each(lambda p: _dot(X[p], H_all[p].astype(BF16)))
        W = each(lambda p: (XS[p][:C] + AV[p][:C]).astype(BF16))
        U = each(lambda p: _dot(Tb[p], blockdiag(W[p])))
        Ub = each(lambda p: U[p].astype(BF16))
        Y = each(lambda p: XS[p][C:] + AV[p][C:] + _dot(Arb[p], blockdiag(Ub[p])))
        UV = each(lambda p: jnp.concatenate([jnp.where(lo_c, Ub[p], Vb[p]), jnp.where(lo_c, Vb[p], Ub[p])], axis=0))
        H_new = each(lambda p: jnp.where(bd, H_all[p] + _dot(ZT[p], UV[p]), 0.0) * gcol[p])
        return Y, H_new

    def group_norm(y):
        def head_mean(t):
            s_lo = jnp.sum(jnp.where(lo_c, t, 0.0), axis=-1, keepdims=True)
            s_hi = jnp.sum(jnp.where(lo_c, 0.0, t), axis=-1, keepdims=True)
            return jnp.where(lo_c, s_lo, s_hi) * (1.0 / N)
        dlt = y - head_mean(y)
        return dlt * lax.rsqrt(head_mean(dlt * dlt) + GN_EPS)

    n_chunks = zr_ref.shape[0] // C
    pres = [state_free(ci) for ci in range(n_chunks)]
    H_all = [state_ref[p] for p in pairs]
    ys = []
    for ci in range(n_chunks):
        Y, H_all = advance(pres[ci], H_all)
        ys.append(jnp.concatenate(each(lambda p: group_norm(Y[p])), axis=-1))
    state_ref[...] = jnp.stack(H_all)
    y = jnp.concatenate(ys, axis=0) * lnw_ref[...] + lnb_ref[...]
    o_ref[...] = ((y + bonus) * g).astype(o_ref.dtype)


def _rwkv(zall, col0, w0, w2, a0, a2, g2, k_k, k_a, r_k, lnx_w, lnx_b, batch, seq):
    m = zall.shape[0]
    d = w0.shape[1]
    tm = CHUNK * CHUNKS_PER_STEP
    per_b = seq // tm
    cb = col0 // d
    row = lambda b, t: (b * per_b + t, 0)
    fixed = lambda b, t: (0, 0)
    zspec = lambda c: pl.BlockSpec((tm, d), lambda b, t: (b * per_b + t, cb + c))
    lspec = pl.BlockSpec((tm, LORA_PAD), lambda b, t: (b * per_b + t, (col0 + 3 * d) // LORA_PAD))
    vec = pl.BlockSpec((1, d), fixed)
    tri = (jnp.arange(CHUNK)[:, None] >= jnp.arange(CHUNK)[None, :]).astype(BF16)
    head_of = jnp.arange(d) // HEAD_DIM
    hsum = (head_of[:, None] == jnp.arange(LANES)[None, :]).astype(BF16)
    hexp = hsum.T
    return pl.pallas_call(
        _rwkv_kernel,
        grid=(batch, per_b),
        in_specs=[zspec(0), zspec(1), zspec(2), lspec,
                  vec, pl.BlockSpec((DECAY_LORA, d), fixed),
                  vec, pl.BlockSpec((AAA_LORA, d), fixed),
                  pl.BlockSpec((GATE_LORA, d), fixed), vec, vec, vec,
                  pl.BlockSpec((CHUNK, CHUNK), fixed), pl.BlockSpec((d, LANES), fixed),
                  pl.BlockSpec((LANES, d), fixed), vec, vec],
        out_specs=pl.BlockSpec((tm, d), row),
        out_shape=jax.ShapeDtypeStruct((m, d), BF16),
        scratch_shapes=[pltpu.VMEM((RWKV_HEADS // 2, 2 * HEAD_DIM, 2 * HEAD_DIM), F32)],
        compiler_params=_params(("parallel", "arbitrary")),
    )(zall, zall, zall, zall, w0, w2, a0, a2, g2, k_k, k_a, r_k, tri, hsum, hexp, lnx_w, lnx_b)


def _mix_kernel(o1, o2, o3, l1, l2, l3, rw_ref, gate_ref, bg_ref, x_ref, gt_ref, hexp_ref, wa_ref, wr_ref, wo_ref,
                out_ref, os_ref, ls_ref):
    tm = x_ref.shape[0]

    def token_order(o_ref, l_ref):
        d = o_ref.shape[1]
        if d == 1:
            return o_ref[0, 0].astype(F32), l_ref[0, 0]
        rows = tm // d
        nct = os_ref.shape[0]
        for r in range(d):
            o = o_ref[0, r].astype(F32)
            for c in range(nct):
                os_ref[c, pl.ds(r, rows, stride=d), :] = o[:, c * LANES:(c + 1) * LANES]
            ls_ref[pl.ds(r, rows, stride=d), :] = l_ref[0, r]
        return jnp.concatenate([os_ref[c] for c in range(nct)], axis=-1), ls_ref[...]

    oa, la = token_order(o1, l1)
    ob, lb = token_order(o2, l2)
    oc, lc = token_order(o3, l3)
    mx = jnp.maximum(jnp.maximum(la, lb), lc)
    ea, eb, ec = jnp.exp(la - mx), jnp.exp(lb - mx), jnp.exp(lc - mx)
    inv = 1.0 / (ea + eb + ec)
    hexp = hexp_ref[...]
    att = (_dot3(ea * inv, hexp) * oa + _dot3(eb * inv, hexp) * ob + _dot3(ec * inv, hexp) * oc)
    y_att = _dot(att.astype(BF16), wa_ref[...])
    y_rwkv = _dot(rw_ref[...], wr_ref[...])
    gates = jax.nn.sigmoid(gate_ref[...].astype(F32) + bg_ref[...])
    d = y_att.shape[1]
    mix = gates[:, :d] * y_att + gates[:, d:] * y_rwkv
    out_ref[...] = x_ref[...] + gt_ref[0] * _dot(mix.astype(BF16), wo_ref[...])


def _mix(att_o, att_l, rw, zall, b_gate, x2, gt1, wa, wr, wo, batch, seq, tm):
    m, d = x2.shape
    per_b = seq // tm
    row = lambda w: pl.BlockSpec((tm, w), lambda i: (i, 0))
    full = lambda a: pl.BlockSpec(a.shape, lambda i: (0, 0))

    def dil(arr):
        dd = arr.shape[1]
        return pl.BlockSpec((1, dd, tm // dd, arr.shape[3]), lambda i: (i // per_b, 0, i % per_b, 0))

    hexp = (jnp.arange(LANES)[:, None] == (jnp.arange(ATT_WIDTH) // HEAD_DIM)[None, :]).astype(BF16)
    return pl.pallas_call(
        _mix_kernel,
        grid=(m // tm,),
        in_specs=[dil(a) for a in att_o] + [dil(a) for a in att_l] +
                 [row(d), row(2 * d), full(b_gate), row(d),
                  pl.BlockSpec((1, 1, d), lambda i: (i // per_b, 0, 0)), full(hexp), full(wa), full(wr), full(wo)],
        out_specs=row(d),
        out_shape=jax.ShapeDtypeStruct((m, d), F32),
        scratch_shapes=[pltpu.VMEM((ATT_WIDTH // LANES, tm, LANES), F32), pltpu.VMEM((tm, LANES), F32)],
        compiler_params=_params(("parallel",)),
    )(*att_o, *att_l, rw, zall, b_gate, x2, gt1, hexp, wa, wr, wo)


HALO = 16


def _ffn_kernel(x_ref, xh_ref, nw_ref, sc_ref, sh_ref, gt_ref, wu_ref, cw_ref, cb_ref, wd_ref, nf_ref, o_ref,
                *, per_b, tf):
    i = pl.program_id(0)
    tm = x_ref.shape[0]
    f = wd_ref.shape[0]
    x = x_ref[...]
    hh = _adaln(xh_ref[...], nw_ref[...], sc_ref[0], sh_ref[0])
    h = jnp.concatenate([jnp.where(i % per_b == 0, 0.0, hh).astype(BF16),
                         _adaln(x, nw_ref[...], sc_ref[0], sh_ref[0]).astype(BF16)], axis=0)

    def conv(u, cols):
        out = cb_ref[:, cols] + cw_ref[2:3, cols] * u[HALO:, :]
        out = out + cw_ref[1:2, cols] * u[HALO - 1:HALO - 1 + tm, :]
        return out + cw_ref[0:1, cols] * u[HALO - 2:HALO - 2 + tm, :]

    acc = None
    for c0 in range(0, f, tf):
        gcols = slice(c0, c0 + tf)
        vcols = slice(f + c0, f + c0 + tf)
        gate = conv(_dot(h, wu_ref[:, gcols]), gcols)
        val = conv(_dot(h, wu_ref[:, vcols]), vcols)
        act = (gate * jax.nn.sigmoid(gate) * val).astype(BF16)
        part = _dot(act, wd_ref[gcols, :])
        acc = part if acc is None else acc + part
    x2 = x + gt_ref[0] * acc
    y = x2 * lax.rsqrt(jnp.mean(x2 * x2, axis=-1, keepdims=True) + RMS_EPS)
    o_ref[...] = y * nf_ref[...]


def _ffn(x1, nw, sc, sh, gt, w_up, conv_w, conv_b, w_down, nf, seq, tm, tf):
    m, d = x1.shape
    f = w_down.shape[0]
    per_b = seq // tm
    hb = tm // HALO
    rowc = lambda i: (i, 0)
    fixed = lambda i: (0, 0)
    bvec = pl.BlockSpec((1, 1, d), lambda i: (i // per_b, 0, 0))
    resident = lambda shape: pl.BlockSpec(shape, fixed, pipeline_mode=pl.Buffered(1))
    return pl.pallas_call(
        functools.partial(_ffn_kernel, per_b=per_b, tf=tf),
        grid=(m // tm,),
        in_specs=[pl.BlockSpec((tm, d), rowc),
                  pl.BlockSpec((HALO, d), lambda i: (jnp.maximum(i * hb - 1, 0), 0)),
                  pl.BlockSpec((1, d), fixed), bvec, bvec, bvec,
                  resident((d, 2 * f)), resident((CONV_WIDTH, 2 * f)), resident((1, 2 * f)), resident((f, d)),
                  pl.BlockSpec((1, d), fixed)],
        out_specs=pl.BlockSpec((tm, d), rowc),
        out_shape=jax.ShapeDtypeStruct((m, d), F32),
        compiler_params=pltpu.CompilerParams(dimension_semantics=("parallel",), vmem_limit_bytes=FFN_VMEM_LIMIT),
    )(x1, x1, nw, sc, sh, gt, w_up, conv_w, conv_b.reshape(1, 2 * f), w_down, nf)


def kernel(x, c, w_ada, b_ada, norm1_w, w_in, b_gate, mu_shift, w0, w2, a0, a2, g2, k_k, k_a, r_k, lnx_w, lnx_b,
           w_att_out, w_rwkv_out, w_o, norm2_w, w_up, conv_w, conv_b, w_down, norm_f_w):
    batch, seq, d = x.shape
    assert w_ada.shape[0] == 1, "the fused ffn kernel applies the final RMSNorm; one layer only"
    l = 0
    grp = 3 * ATT_WIDTH
    att_in = len(ATT_PATTERNS) * grp
    lora = DECAY_LORA + AAA_LORA + GATE_LORA
    xf = x.reshape(batch * seq, d)

    ada = _ada(c, w_ada[l], b_ada[l])
    sh1, sc1, gt1, sh2, sc2, gt2 = [t.reshape(batch, 1, d) for t in jnp.split(ada, 6, axis=-1)]
    h = _norm(xf, norm1_w[l].reshape(1, d), sc1, sh1, seq, 1024)

    win = w_in[l].astype(BF16)
    w_rest = jnp.concatenate(
        [win[:, att_in + 3 * d + lora:], win[:, att_in:att_in + 3 * d + lora],
         jnp.zeros((d, LORA_PAD - lora), BF16)], axis=1)
    mu_all = jnp.concatenate([jnp.zeros((2 * d,), F32), mu_shift[l], jnp.zeros((LORA_PAD - lora,), F32)])
    zall = _mm_shift(h, w_rest, mu_all.reshape(1, -1), BF16, seq, 2048, PROJ_TN)

    att_o, att_l = [], []
    for gi, (_, dilation) in enumerate(ATT_PATTERNS):
        qkv = _mm_dilate(h, win, gi * grp, grp, HEAD_DIM ** -0.5, dilation, batch, seq, 2048, ATT_WIDTH)
        o, lse = _attention(qkv.reshape(batch * seq, grp), seq // dilation // ATT_BLOCK)
        att_o.append(o.reshape(batch, dilation, seq // dilation, ATT_WIDTH))
        att_l.append(lse.reshape(batch, dilation, seq // dilation, LANES))

    vec = lambda t: t.reshape(1, d)
    rw = _rwkv(zall, 2 * d, vec(w0[l]), w2[l].astype(BF16), vec(a0[l]), a2[l].astype(BF16), g2[l].astype(BF16),
               vec(k_k[l]), vec(k_a[l]), vec(r_k[l]), vec(lnx_w[l]), vec(lnx_b[l]), batch, seq)

    x1 = _mix(att_o, att_l, rw, zall, b_gate[l].reshape(1, 2 * d), xf, gt1,
              w_att_out[l].astype(BF16), w_rwkv_out[l].astype(BF16), w_o[l].astype(BF16), batch, seq, 512)
    out = _ffn(x1, norm2_w[l].reshape(1, d), sc2, sh2, gt2, w_up[l].astype(BF16), conv_w[l], conv_b[l],
               w_down[l].astype(BF16), norm_f_w.reshape(1, d), seq, 512, 1408)
    return out.reshape(batch, seq, d)
```

```python
import functools
import math

import jax
import jax.numpy as jnp
from jax import lax
from jax.experimental import pallas as pl
from jax.experimental.pallas import tpu as pltpu

F32 = jnp.float32
BF16 = jnp.bfloat16

LANES = 128
ATT_PATTERNS = ((128, 1), (512, 4), (2048, 16))
ATT_HEADS = 8
HEAD_DIM = 64
ATT_WIDTH = ATT_HEADS * HEAD_DIM
ATT_BLOCK = 128
ATT_TQ = 512
RWKV_HEADS = 16
DECAY_LORA = 64
AAA_LORA = 64
GATE_LORA = 160
LORA_PAD = 512
SHIFT_SLABS = 4
ADA_TN = 1536
NORM_TM = 1024
ZALL_TM, ZALL_TN = 1024, 2816
QKV_TM, QKV_TN = 1024, 3 * ATT_WIDTH
MIX_TM = 512
FFN_TM, FFN_TF = 512, 1408
CONV_WIDTH = 3
RMS_EPS = 1e-6
GN_EPS = 64e-5
DECAY_SCALE = math.exp(-0.5)
CHUNK = 64
CHUNKS_PER_STEP = 4
VMEM_LIMIT = 48 * 1024 * 1024
FFN_VMEM_LIMIT = 56 * 1024 * 1024


def _dot(a, b):
    return jnp.dot(a, b, preferred_element_type=F32)


def _dot_nt(a, b):
    return lax.dot_general(a, b, (((1,), (1,)), ((), ())), preferred_element_type=F32)


def _split3(x):
    hi = x.astype(BF16)
    r1 = x - hi.astype(F32)
    mid = r1.astype(BF16)
    lo = (r1 - mid.astype(F32)).astype(BF16)
    return hi, mid, lo


def _dot2(x, w):
    hi = x.astype(BF16)
    lo = (x - hi.astype(F32)).astype(BF16)
    return _dot(hi, w) + _dot(lo, w)


def _params(sem):
    return pltpu.CompilerParams(dimension_semantics=sem, vmem_limit_bytes=VMEM_LIMIT)


def _ada_kernel(c_ref, w_ref, b_ref, o_ref):
    ch, cl, _ = _split3(c_ref[...])
    wh, wl, _ = _split3(w_ref[...])
    o_ref[...] = _dot(ch, wh) + _dot(cl, wh) + _dot(ch, wl) + b_ref[...]


def _ada(c, w_ada, b_ada):
    b, d = c.shape
    n = w_ada.shape[1]
    tn = ADA_TN
    return pl.pallas_call(
        _ada_kernel,
        grid=(n // tn,),
        in_specs=[pl.BlockSpec((b, d), lambda j: (0, 0)),
                  pl.BlockSpec((d, tn), lambda j: (0, j)),
                  pl.BlockSpec((1, tn), lambda j: (0, j))],
        out_specs=pl.BlockSpec((b, tn), lambda j: (0, j)),
        out_shape=jax.ShapeDtypeStruct((b, n), F32),
        compiler_params=_params(("arbitrary",)),
    )(c, w_ada, b_ada.reshape(1, n))


def _adaln(x, nw, sc, sh):
    y = x * lax.rsqrt(jnp.mean(x * x, axis=-1, keepdims=True) + RMS_EPS)
    return (y * nw) * (1.0 + sc) + sh


def _norm_kernel(x_ref, nw_ref, sc_ref, sh_ref, o_ref):
    o_ref[...] = _adaln(x_ref[...], nw_ref[...], sc_ref[0], sh_ref[0]).astype(o_ref.dtype)


def _norm(x2, nw, sc, sh, seq, tm):
    m, d = x2.shape
    per_b = seq // tm
    bvec = pl.BlockSpec((1, 1, d), lambda i: (i // per_b, 0, 0))
    return pl.pallas_call(
        _norm_kernel,
        grid=(m // tm,),
        in_specs=[pl.BlockSpec((tm, d), lambda i: (i, 0)), pl.BlockSpec((1, d), lambda i: (0, 0)), bvec, bvec],
        out_specs=pl.BlockSpec((tm, d), lambda i: (i, 0)),
        out_shape=jax.ShapeDtypeStruct((m, d), BF16),
        compiler_params=_params(("parallel",)),
    )(x2, nw, sc, sh)


def _mm_shift_kernel(a_ref, w_ref, mu_ref, o_ref, carry_ref, *, per_b):
    i = pl.program_id(0)
    j = pl.program_id(1)
    tm = a_ref.shape[0]

    @pl.when((i == 0) & (j == 0))
    def _():
        carry_ref[...] = jnp.zeros_like(carry_ref)

    before = jnp.where(i % per_b == 0, 0.0, carry_ref[j])
    w = w_ref[...]
    mu = mu_ref[...]
    slab = tm // SHIFT_SLABS
    first8 = lax.broadcasted_iota(jnp.int32, (8, w.shape[1]), 0) == 0
    outs = []
    for s in range(SHIFT_SLABS):
        z = _dot(a_ref[s * slab:(s + 1) * slab, :], w)
        rolled = pltpu.roll(z, 1, axis=0)
        prev = jnp.concatenate([jnp.where(first8, before, rolled[:8]), rolled[8:]], axis=0)
        outs.append((z + (prev - z) * mu).astype(o_ref.dtype))
        before = z[slab - 1:slab, :]
    carry_ref[j] = before
    o_ref[...] = jnp.concatenate(outs, axis=0)


def _mm_shift(a, w, mu, out_dtype, seq, tm, tn):
    m, k = a.shape
    n = w.shape[1]
    return pl.pallas_call(
        functools.partial(_mm_shift_kernel, per_b=seq // tm),
        grid=(m // tm, n // tn),
        in_specs=[pl.BlockSpec((tm, k), lambda i, j: (i, 0)), pl.BlockSpec((k, tn), lambda i, j: (0, j)),
                  pl.BlockSpec((1, tn), lambda i, j: (0, j))],
        out_specs=pl.BlockSpec((tm, tn), lambda i, j: (i, j)),
        out_shape=jax.ShapeDtypeStruct((m, n), out_dtype),
        scratch_shapes=[pltpu.VMEM((n // tn, 1, tn), F32)],
        compiler_params=_params(("arbitrary", "arbitrary")),
    )(a, w, mu)


MAX_ROW_STRIDE = 4


def _mm_dilate_kernel(a_ref, w_ref, cs_ref, o_ref, acc_ref, tmp_ref, *, dilation):
    tm = a_ref.shape[0]
    rows = tm // dilation
    acc = _dot(a_ref[...], w_ref[...]) * cs_ref[...]
    if dilation == 1:
        o_ref[0, 0] = acc.astype(o_ref.dtype)
        return
    nct = acc.shape[1] // LANES
    for c in range(nct):
        acc_ref[c] = acc[:, c * LANES:(c + 1) * LANES]

    def emit(r, src_ref, start, stride):
        o_ref[0, r] = jnp.concatenate(
            [src_ref[c, pl.ds(start, rows, stride=stride), :] for c in range(nct)], axis=-1).astype(o_ref.dtype)

    if dilation <= MAX_ROW_STRIDE:
        for r in range(dilation):
            emit(r, acc_ref, r, dilation)
        return
    s1 = MAX_ROW_STRIDE
    s2 = dilation // s1
    part = tm // s1
    for q in range(s1):
        for c in range(nct):
            tmp_ref[c, q * part:(q + 1) * part, :] = acc_ref[c, pl.ds(q, part, stride=s1), :]
    for q in range(s1):
        for q2 in range(s2):
            emit(q + s1 * q2, tmp_ref, q * part + q2, s2)


def _mm_dilate(a, w, col0, col_scale, dilation, batch, seq, tm, tn):
    m, k = a.shape
    n = col_scale.shape[1]
    cb = col0 // tn
    per_b = seq // tm
    rows = tm // dilation
    return pl.pallas_call(
        functools.partial(_mm_dilate_kernel, dilation=dilation),
        grid=(m // tm, n // tn),
        in_specs=[pl.BlockSpec((tm, k), lambda i, j: (i, 0)), pl.BlockSpec((k, tn), lambda i, j: (0, cb + j)),
                  pl.BlockSpec((1, tn), lambda i, j: (0, j))],
        out_specs=pl.BlockSpec((1, dilation, rows, tn), lambda i, j: (i // per_b, 0, i % per_b, j)),
        out_shape=jax.ShapeDtypeStruct((batch, dilation, seq // dilation, n), BF16),
        scratch_shapes=[pltpu.VMEM((tn // LANES, tm, LANES), F32)] * 2,
        compiler_params=_params(("parallel", "arbitrary")),
    )(a, w, col_scale)


def _att_kernel(q_ref, k_ref, kh_ref, v_ref, vh_ref, o_ref, l_ref, *, blocks_per_seq):
    i = pl.program_id(0)
    nq = q_ref.shape[0] // ATT_BLOCK
    qi = lax.broadcasted_iota(jnp.int32, (ATT_BLOCK, 2 * ATT_BLOCK), 0)
    kj = lax.broadcasted_iota(jnp.int32, (ATT_BLOCK, 2 * ATT_BLOCK), 1)
    band = (kj >= qi) & (kj <= qi + ATT_BLOCK)
    neg = jnp.float32(-1e30)
    bias_std = jnp.where(band, 0.0, neg)
    bias_first = jnp.where(band & (kj >= ATT_BLOCK), 0.0, neg)
    lane = lax.broadcasted_iota(jnp.int32, (ATT_BLOCK, LANES), 1)
    o_blocks, l_blocks = [], []
    for qb in range(nq):
        rows = slice(qb * ATT_BLOCK, (qb + 1) * ATT_BLOCK)
        prev = slice((qb - 1) * ATT_BLOCK, qb * ATT_BLOCK)
        first = (i * nq + qb) % blocks_per_seq == 0
        bias = jnp.where(first, bias_first, bias_std)
        q = q_ref[rows, :]
        k = jnp.concatenate([kh_ref[...] if qb == 0 else k_ref[prev, :], k_ref[rows, :]], axis=0)
        v = jnp.concatenate([vh_ref[...] if qb == 0 else v_ref[prev, :], v_ref[rows, :]], axis=0)
        outs = []
        lse_tile = jnp.zeros((ATT_BLOCK, LANES), F32)
        for hp in range(ATT_HEADS // 2):
            tile = slice(hp * LANES, (hp + 1) * LANES)
            q2, k2, v2 = q[:, tile], k[:, tile], v[:, tile]
            halves = []
            for side in range(2):
                own = (lane < HEAD_DIM) if side == 0 else (lane >= HEAD_DIM)
                s = _dot_nt(jnp.where(own, q2, jnp.zeros((), q2.dtype)), k2) + bias
                m = jnp.max(s, axis=-1, keepdims=True)
                p = jnp.exp(s - m)
                den = jnp.sum(p, axis=-1, keepdims=True)
                halves.append(_dot(p.astype(BF16), v2) / den)
                lse_tile = jnp.where(lane == 2 * hp + side, m + jnp.log(den), lse_tile)
            outs.append(jnp.where(lane < HEAD_DIM, halves[0], halves[1]))
        o_blocks.append(jnp.concatenate(outs, axis=-1).astype(o_ref.dtype))
        l_blocks.append(lse_tile)
    o_ref[...] = jnp.concatenate(o_blocks, axis=0)
    l_ref[...] = jnp.concatenate(l_blocks, axis=0)


def _attention(qkv, blocks_per_seq):
    rows = qkv.shape[0]
    tq = ATT_TQ
    hb = tq // ATT_BLOCK
    main = lambda c: pl.BlockSpec((tq, ATT_WIDTH), lambda i: (i, c))
    halo = lambda c: pl.BlockSpec((ATT_BLOCK, ATT_WIDTH), lambda i: (jnp.maximum(i * hb - 1, 0), c))
    return pl.pallas_call(
        functools.partial(_att_kernel, blocks_per_seq=blocks_per_seq),
        grid=(rows // tq,),
        in_specs=[main(0), main(1), halo(1), main(2), halo(2)],
        out_specs=[pl.BlockSpec((tq, ATT_WIDTH), lambda i: (i, 0)), pl.BlockSpec((tq, LANES), lambda i: (i, 0))],
        out_shape=[jax.ShapeDtypeStruct((rows, ATT_WIDTH), BF16), jax.ShapeDtypeStruct((rows, LANES), F32)],
        compiler_params=_params(("parallel",)),
    )(qkv, qkv, qkv, qkv, qkv)


def _rwkv_kernel(zr_ref, zk_ref, zv_ref, zl_ref, w0_ref, w2_ref, a0_ref, a2_ref, g2_ref, kk_ref, ka_ref, rk_ref,
                 tri_ref, hsum_ref, hexp_ref, lnw_ref, lnb_ref, o_ref, state_ref):
    C = CHUNK
    N = HEAD_DIM

    @pl.when(pl.program_id(1) == 0)
    def _():
        state_ref[...] = jnp.zeros_like(state_ref)

    r = zr_ref[...].astype(F32)
    k = zk_ref[...].astype(F32)
    v = zv_ref[...].astype(F32)
    zl = zl_ref[...].astype(F32)
    w_low = zl[:, :DECAY_LORA]
    a_low = zl[:, DECAY_LORA:DECAY_LORA + AAA_LORA]
    g_low = zl[:, DECAY_LORA + AAA_LORA:DECAY_LORA + AAA_LORA + GATE_LORA]

    wpre = w0_ref[...] + _dot(jnp.tanh(w_low).astype(BF16), w2_ref[...])
    lw = jax.nn.sigmoid(wpre) * (-DECAY_SCALE)
    a = jax.nn.sigmoid(a0_ref[...] + _dot(a_low.astype(BF16), a2_ref[...]))
    g = _dot(jax.nn.sigmoid(g_low).astype(BF16), g2_ref[...])
    km = k * (1.0 + (a - 1.0) * ka_ref[...])

    kn = k * kk_ref[...]
    inv = lax.rsqrt(jnp.maximum(_dot2(kn * kn, hsum_ref[...]), 1e-24))
    kn = kn * _dot2(inv, hexp_ref[...])
    rk_sum = _dot((r * km * rk_ref[...]).astype(BF16), hsum_ref[...])
    bonus = _dot(rk_sum.astype(BF16), hexp_ref[...]) * v
    vb = v.astype(BF16)

    P2 = 2 * N
    lane = lambda shape: lax.broadcasted_iota(jnp.int32, shape, 1)
    row = lambda shape: lax.broadcasted_iota(jnp.int32, shape, 0)
    lo_c = lane((C, P2)) < N
    bd = (row((P2, P2)) < N) == (lane((P2, P2)) < N)
    ri = row((2 * C, 4 * C))
    cj = lane((2 * C, 4 * C)) & (C - 1)
    keep = cj < (ri & (C - 1)) + jnp.where(ri >= C, 1, 0)
    eye2 = ((lane((C, P2)) & (N - 1)) == row((C, P2))).astype(F32)
    zero_b = jnp.zeros((), BF16)
    odd_head = (lane((C, r.shape[1])) & N) != 0
    l1 = lw.astype(BF16)
    l2 = (lw - l1.astype(F32)).astype(BF16)
    cum_all = _dot(tri_ref[...], l1) + _dot(tri_ref[...], l2)

    def blockdiag(m):
        return jnp.where(bd, jnp.concatenate([m, m], axis=0), jnp.zeros((), m.dtype))

    pairs = range(RWKV_HEADS // 2)
    each = lambda f: [f(p) for p in pairs]
    tl = lambda p: slice(p * P2, (p + 1) * P2)

    def state_free(ci):
        rows = slice(ci * C, (ci + 1) * C)
        lwc = lw[rows]
        cum = cum_all[rows]
        e_pos = jnp.exp(cum)
        e_neg = jnp.exp(-cum)
        knc = kn[rows]
        xa = (-knc * jnp.exp(cum - lwc)).astype(BF16)
        xr = (r[rows] * e_pos).astype(BF16)
        zk = km[rows] * e_neg
        zb = knc * a[rows] * e_neg
        zt = jnp.concatenate([jnp.where(odd_head, zk, zb), jnp.where(odd_head, zb, zk)], axis=0).T.astype(BF16)
        glast = e_pos[C - 1:C, :]

        X = each(lambda p: jnp.concatenate([xa[:, tl(p)], xr[:, tl(p)]], axis=0))
        ZT = each(lambda p: zt[tl(p), :])
        Vb = each(lambda p: vb[rows, tl(p)])
        A = each(lambda p: jnp.where(keep, _dot(X[p], jnp.concatenate(
            [jnp.where(bd, ZT[p], zero_b), jnp.where(bd, zero_b, ZT[p])], axis=1)), 0.0))
        Lb = each(lambda p: A[p][:C, :P2].astype(BF16))
        Pm = each(lambda p: _dot(Lb[p], blockdiag(Lb[p])))
        T = each(lambda p: eye2 + A[p][:C, :P2])
        span = 2
        while 2 * span < C:
            R = each(lambda p: _dot(jnp.concatenate([Pm[p], T[p]], axis=0).astype(BF16),
                                    blockdiag(Pm[p].astype(BF16))))
            Pm = each(lambda p: R[p][:C])
            T = each(lambda p: T[p] + R[p][C:])
            span *= 2
        Tb = each(lambda p: (T[p] + _dot(T[p].astype(BF16), blockdiag(Pm[p].astype(BF16)))).astype(BF16))
        XA = each(lambda p: jnp.concatenate([X[p], A[p][:, P2:].astype(BF16)], axis=1))
        Vad = each(lambda p: jnp.where(bd, zero_b, jnp.concatenate([Vb[p], Vb[p]], axis=0)))
        Arb = each(lambda p: A[p][C:, :P2].astype(BF16))
        gcol = each(lambda p: jnp.broadcast_to(glast[:, tl(p)], (P2, P2)).T)
        return XA, Vad, ZT, Vb, Tb, Arb, gcol

    def advance(pre, H_all):
        XA, Vad, ZT, Vb, Tb, Arb, gcol = pre
        XS = each(lambda p: _dot(XA[p], jnp.concatenate([H_all[p].astype(BF16), Vad[p]], axis=0)))
        W = each(lambda p: XS[p][:C].astype(BF16))
        U = each(lambda p: _dot(Tb[p], blockdiag(W[p])))
        Ub = each(lambda p: U[p].astype(BF16))
        Y = each(lambda p: XS[p][C:] + _dot(Arb[p], blockdiag(Ub[p])))
        UV = each(lambda p: jnp.concatenate([jnp.where(lo_c, Ub[p], Vb[p]), jnp.where(lo_c, Vb[p], Ub[p])], axis=0))
        H_new = each(lambda p: jnp.where(bd, H_all[p] + _dot(ZT[p], UV[p]), 0.0) * gcol[p])
        return Y, H_new

    def group_norm(y):
        def head_mean(t):
            s_lo = jnp.sum(jnp.where(lo_c, t, 0.0), axis=-1, keepdims=True)
            s_hi = jnp.sum(jnp.where(lo_c, 0.0, t), axis=-1, keepdims=True)
            return jnp.where(lo_c, s_lo, s_hi) * (1.0 / N)
        dlt = y - head_mean(y)
        return dlt * lax.rsqrt(head_mean(dlt * dlt) + GN_EPS)

    n_chunks = zr_ref.shape[0] // C
    pres = [state_free(ci) for ci in range(n_chunks)]
    H_all = [state_ref[p] for p in pairs]
    ys = []
    for ci in range(n_chunks):
        Y, H_all = advance(pres[ci], H_all)
        ys.append(jnp.concatenate(each(lambda p: group_norm(Y[p])), axis=-1))
    state_ref[...] = jnp.stack(H_all)
    y = jnp.concatenate(ys, axis=0) * lnw_ref[...] + lnb_ref[...]
    o_ref[...] = ((y + bonus) * g).astype(o_ref.dtype)


def _rwkv(zall, col0, w0, w2, a0, a2, g2, k_k, k_a, r_k, lnx_w, lnx_b, batch, seq):
    m = zall.shape[0]
    d = w0.shape[1]
    tm = CHUNK * CHUNKS_PER_STEP
    per_b = seq // tm
    cb = col0 // d
    row = lambda b, t: (b * per_b + t, 0)
    fixed = lambda b, t: (0, 0)
    zspec = lambda c: pl.BlockSpec((tm, d), lambda b, t: (b * per_b + t, cb + c))
    lspec = pl.BlockSpec((tm, LORA_PAD), lambda b, t: (b * per_b + t, (col0 + 3 * d) // LORA_PAD))
    vec = pl.BlockSpec((1, d), fixed)
    t_idx = jnp.arange(tm)
    tri = ((t_idx[:, None] >= t_idx[None, :]) &
           (t_idx[:, None] // CHUNK == t_idx[None, :] // CHUNK)).astype(BF16)
    head_of = jnp.arange(d) // HEAD_DIM
    hsum = (head_of[:, None] == jnp.arange(LANES)[None, :]).astype(BF16)
    hexp = hsum.T
    return pl.pallas_call(
        _rwkv_kernel,
        grid=(batch, per_b),
        in_specs=[zspec(0), zspec(1), zspec(2), lspec,
                  vec, pl.BlockSpec((DECAY_LORA, d), fixed),
                  vec, pl.BlockSpec((AAA_LORA, d), fixed),
                  pl.BlockSpec((GATE_LORA, d), fixed), vec, vec, vec,
                  pl.BlockSpec((tm, tm), fixed), pl.BlockSpec((d, LANES), fixed),
                  pl.BlockSpec((LANES, d), fixed), vec, vec],
        out_specs=pl.BlockSpec((tm, d), row),
        out_shape=jax.ShapeDtypeStruct((m, d), BF16),
        scratch_shapes=[pltpu.VMEM((RWKV_HEADS // 2, 2 * HEAD_DIM, 2 * HEAD_DIM), F32)],
        compiler_params=_params(("parallel", "arbitrary")),
    )(zall, zall, zall, zall, w0, w2, a0, a2, g2, k_k, k_a, r_k, tri, hsum, hexp, lnx_w, lnx_b)


def _mix_kernel(o1, o2, o3, l1, l2, l3, rw_ref, gate_ref, bg_ref, x_ref, gt_ref, hexp_ref, wa_ref, wr_ref, wo_ref,
                out_ref, os_ref, ls_ref):
    tm = x_ref.shape[0]

    def token_order(o_ref, l_ref):
        d = o_ref.shape[1]
        if d == 1:
            return o_ref[0, 0].astype(F32), l_ref[0, 0]
        rows = tm // d
        nct = os_ref.shape[0]
        for r in range(d):
            o = o_ref[0, r].astype(F32)
            for c in range(nct):
                os_ref[c, pl.ds(r, rows, stride=d), :] = o[:, c * LANES:(c + 1) * LANES]
            ls_ref[pl.ds(r, rows, stride=d), :] = l_ref[0, r]
        return jnp.concatenate([os_ref[c] for c in range(nct)], axis=-1), ls_ref[...]

    oa, la = token_order(o1, l1)
    ob, lb = token_order(o2, l2)
    oc, lc = token_order(o3, l3)
    mx = jnp.maximum(jnp.maximum(la, lb), lc)
    ea, eb, ec = jnp.exp(la - mx), jnp.exp(lb - mx), jnp.exp(lc - mx)
    inv = 1.0 / (ea + eb + ec)
    hexp = hexp_ref[...]
    att = (_dot2(ea * inv, hexp) * oa + _dot2(eb * inv, hexp) * ob + _dot2(ec * inv, hexp) * oc)
    y_att = _dot(att.astype(BF16), wa_ref[...])
    y_rwkv = _dot(rw_ref[...], wr_ref[...])
    gates = jax.nn.sigmoid(gate_ref[...].astype(F32) + bg_ref[...])
    d = y_att.shape[1]
    mix = gates[:, :d] * y_att + gates[:, d:] * y_rwkv
    out_ref[...] = x_ref[...] + gt_ref[0] * _dot(mix.astype(BF16), wo_ref[...])


def _mix(att_o, att_l, rw, zall, b_gate, x2, gt1, wa, wr, wo, batch, seq, tm):
    m, d = x2.shape
    per_b = seq // tm
    row = lambda w: pl.BlockSpec((tm, w), lambda i: (i, 0))
    full = lambda a: pl.BlockSpec(a.shape, lambda i: (0, 0))

    def dil(arr):
        dd = arr.shape[1]
        return pl.BlockSpec((1, dd, tm // dd, arr.shape[3]), lambda i: (i // per_b, 0, i % per_b, 0))

    hexp = (jnp.arange(LANES)[:, None] == (jnp.arange(ATT_WIDTH) // HEAD_DIM)[None, :]).astype(BF16)
    return pl.pallas_call(
        _mix_kernel,
        grid=(m // tm,),
        in_specs=[dil(a) for a in att_o] + [dil(a) for a in att_l] +
                 [row(d), row(2 * d), full(b_gate), row(d),
                  pl.BlockSpec((1, 1, d), lambda i: (i // per_b, 0, 0)), full(hexp), full(wa), full(wr), full(wo)],
        out_specs=row(d),
        out_shape=jax.ShapeDtypeStruct((m, d), F32),
        scratch_shapes=[pltpu.VMEM((ATT_WIDTH // LANES, tm, LANES), F32), pltpu.VMEM((tm, LANES), F32)],
        compiler_params=_params(("parallel",)),
    )(*att_o, *att_l, rw, zall, b_gate, x2, gt1, hexp, wa, wr, wo)


HALO = 16


def _ffn_kernel(x_ref, xh_ref, nw_ref, sc_ref, sh_ref, gt_ref, wu_ref, cw_ref, cb_ref, wd_ref, nf_ref, o_ref,
                *, per_b, tf):
    i = pl.program_id(0)
    tm = x_ref.shape[0]
    f = wd_ref.shape[0]
    x = x_ref[...]
    hh = _adaln(xh_ref[...], nw_ref[...], sc_ref[0], sh_ref[0])
    h = jnp.concatenate([jnp.where(i % per_b == 0, 0.0, hh).astype(BF16),
                         _adaln(x, nw_ref[...], sc_ref[0], sh_ref[0]).astype(BF16)], axis=0)

    def conv(u, cols):
        out = cb_ref[:, cols] + cw_ref[2:3, cols] * u[HALO:, :]
        out = out + cw_ref[1:2, cols] * u[HALO - 1:HALO - 1 + tm, :]
        return out + cw_ref[0:1, cols] * u[HALO - 2:HALO - 2 + tm, :]

    acc = None
    for c0 in range(0, f, tf):
        gcols = slice(c0, c0 + tf)
        vcols = slice(f + c0, f + c0 + tf)
        gate = conv(_dot(h, wu_ref[:, gcols]), gcols)
        val = conv(_dot(h, wu_ref[:, vcols]), vcols)
        act = (gate * jax.nn.sigmoid(gate) * val).astype(BF16)
        part = _dot(act, wd_ref[gcols, :])
        acc = part if acc is None else acc + part
    x2 = x + gt_ref[0] * acc
    y = x2 * lax.rsqrt(jnp.mean(x2 * x2, axis=-1, keepdims=True) + RMS_EPS)
    o_ref[...] = y * nf_ref[...]


def _ffn(x1, nw, sc, sh, gt, w_up, conv_w, conv_b, w_down, nf, seq, tm, tf):
    m, d = x1.shape
    f = w_down.shape[0]
    per_b = seq // tm
    hb = tm // HALO
    rowc = lambda i: (i, 0)
    fixed = lambda i: (0, 0)
    bvec = pl.BlockSpec((1, 1, d), lambda i: (i // per_b, 0, 0))
    resident = lambda shape: pl.BlockSpec(shape, fixed, pipeline_mode=pl.Buffered(1))
    return pl.pallas_call(
        functools.partial(_ffn_kernel, per_b=per_b, tf=tf),
        grid=(m // tm,),
        in_specs=[pl.BlockSpec((tm, d), rowc),
                  pl.BlockSpec((HALO, d), lambda i: (jnp.maximum(i * hb - 1, 0), 0)),
                  pl.BlockSpec((1, d), fixed), bvec, bvec, bvec,
                  resident((d, 2 * f)), resident((CONV_WIDTH, 2 * f)), resident((1, 2 * f)), resident((f, d)),
                  pl.BlockSpec((1, d), fixed)],
        out_specs=pl.BlockSpec((tm, d), rowc),
        out_shape=jax.ShapeDtypeStruct((m, d), F32),
        compiler_params=pltpu.CompilerParams(dimension_semantics=("parallel",), vmem_limit_bytes=FFN_VMEM_LIMIT),
    )(x1, x1, nw, sc, sh, gt, w_up, conv_w, conv_b.reshape(1, 2 * f), w_down, nf)


def kernel(x, c, w_ada, b_ada, norm1_w, w_in, b_gate, mu_shift, w0, w2, a0, a2, g2, k_k, k_a, r_k, lnx_w, lnx_b,
           w_att_out, w_rwkv_out, w_o, norm2_w, w_up, conv_w, conv_b, w_down, norm_f_w):
    batch, seq, d = x.shape
    assert w_ada.shape[0] == 1, "the fused ffn kernel applies the final RMSNorm; one layer only"
    l = 0
    grp = 3 * ATT_WIDTH
    att_in = len(ATT_PATTERNS) * grp
    lora = DECAY_LORA + AAA_LORA + GATE_LORA
    xf = x.reshape(batch * seq, d)

    ada = _ada(c, w_ada[l], b_ada[l])
    sh1, sc1, gt1, sh2, sc2, gt2 = [t.reshape(batch, 1, d) for t in jnp.split(ada, 6, axis=-1)]
    h = _norm(xf, norm1_w[l].reshape(1, d), sc1, sh1, seq, NORM_TM)

    win = w_in[l].astype(BF16)
    w_rest = jnp.concatenate(
        [win[:, att_in + 3 * d + lora:], win[:, att_in:att_in + 3 * d + lora],
         jnp.zeros((d, LORA_PAD - lora), BF16)], axis=1)
    mu_all = jnp.concatenate([jnp.zeros((2 * d,), F32), mu_shift[l], jnp.zeros((LORA_PAD - lora,), F32)])
    zall = _mm_shift(h, w_rest, mu_all.reshape(1, -1), BF16, seq, ZALL_TM, ZALL_TN)

    att_o, att_l = [], []
    qkv_scale = jnp.concatenate([jnp.full((ATT_WIDTH,), HEAD_DIM ** -0.5, F32),
                                 jnp.ones((2 * ATT_WIDTH,), F32)]).reshape(1, grp)
    for gi, (_, dilation) in enumerate(ATT_PATTERNS):
        qkv = _mm_dilate(h, win, gi * grp, qkv_scale, dilation, batch, seq, QKV_TM, QKV_TN)
        o, lse = _attention(qkv.reshape(batch * seq, grp), seq // dilation // ATT_BLOCK)
        att_o.append(o.reshape(batch, dilation, seq // dilation, ATT_WIDTH))
        att_l.append(lse.reshape(batch, dilation, seq // dilation, LANES))

    vec = lambda t: t.reshape(1, d)
    rw = _rwkv(zall, 2 * d, vec(w0[l]), w2[l].astype(BF16), vec(a0[l]), a2[l].astype(BF16), g2[l].astype(BF16),
               vec(k_k[l]), vec(k_a[l]), vec(r_k[l]), vec(lnx_w[l]), vec(lnx_b[l]), batch, seq)

    x1 = _mix(att_o, att_l, rw, zall, b_gate[l].reshape(1, 2 * d), xf, gt1,
              w_att_out[l].astype(BF16), w_rwkv_out[l].astype(BF16), w_o[l].astype(BF16), batch, seq, MIX_TM)
    out = _ffn(x1, norm2_w[l].reshape(1, d), sc2, sh2, gt2, w_up[l].astype(BF16), conv_w[l], conv_b[l],
               w_down[l].astype(BF16), norm_f_w.reshape(1, d), seq, FFN_TM, FFN_TF)
    return out.reshape(batch, seq, d)
```

```python
import functools
import math

import jax
import jax.numpy as jnp
from jax import lax
from jax.experimental import pallas as pl
from jax.experimental.pallas import tpu as pltpu

F32 = jnp.float32
BF16 = jnp.bfloat16

LANES = 128
ATT_PATTERNS = ((128, 1), (512, 4), (2048, 16))
ATT_HEADS = 8
HEAD_DIM = 64
ATT_WIDTH = ATT_HEADS * HEAD_DIM
ATT_BLOCK = 128
ATT_TQ = 1024
RWKV_HEADS = 16
DECAY_LORA = 64
AAA_LORA = 64
GATE_LORA = 160
LORA_PAD = 512
SHIFT_SLABS = 4
ADA_TN = 1536
NORM_TM = 2048
ZALL_TM, ZALL_TN = 1024, 2816
QKV_TM, QKV_TN = 1024, 3 * ATT_WIDTH
MIX_TM = 512
FFN_TM, FFN_TF = 512, 2816
CONV_WIDTH = 3
RMS_EPS = 1e-6
GN_EPS = 64e-5
DECAY_SCALE = math.exp(-0.5)
CHUNK = 64
PAIRS_PER_GROUP = 8
CHUNKS_PER_STEP = 4
VMEM_LIMIT = 48 * 1024 * 1024
FFN_VMEM_LIMIT = 56 * 1024 * 1024


def _dot(a, b):
    return jnp.dot(a, b, preferred_element_type=F32)


def _dot_nt(a, b):
    return lax.dot_general(a, b, (((1,), (1,)), ((), ())), preferred_element_type=F32)


def _split3(x):
    hi = x.astype(BF16)
    r1 = x - hi.astype(F32)
    mid = r1.astype(BF16)
    lo = (r1 - mid.astype(F32)).astype(BF16)
    return hi, mid, lo


def _dot2(x, w):
    hi = x.astype(BF16)
    lo = (x - hi.astype(F32)).astype(BF16)
    return _dot(hi, w) + _dot(lo, w)


def _params(sem):
    return pltpu.CompilerParams(dimension_semantics=sem, vmem_limit_bytes=VMEM_LIMIT)


def _ada_kernel(c_ref, w_ref, b_ref, o_ref):
    ch, cl, _ = _split3(c_ref[...])
    wh, wl, _ = _split3(w_ref[...])
    o_ref[...] = _dot(ch, wh) + _dot(cl, wh) + _dot(ch, wl) + b_ref[...]


def _ada(c, w_ada, b_ada):
    b, d = c.shape
    n = w_ada.shape[1]
    tn = ADA_TN
    return pl.pallas_call(
        _ada_kernel,
        grid=(n // tn,),
        in_specs=[pl.BlockSpec((b, d), lambda j: (0, 0)),
                  pl.BlockSpec((d, tn), lambda j: (0, j)),
                  pl.BlockSpec((1, tn), lambda j: (0, j))],
        out_specs=pl.BlockSpec((b, tn), lambda j: (0, j)),
        out_shape=jax.ShapeDtypeStruct((b, n), F32),
        compiler_params=_params(("arbitrary",)),
    )(c, w_ada, b_ada.reshape(1, n))


def _adaln(x, nw, sc, sh):
    y = x * lax.rsqrt(jnp.mean(x * x, axis=-1, keepdims=True) + RMS_EPS)
    return (y * nw) * (1.0 + sc) + sh


def _norm_kernel(x_ref, nw_ref, sc_ref, sh_ref, o_ref):
    o_ref[...] = _adaln(x_ref[...], nw_ref[...], sc_ref[0], sh_ref[0]).astype(o_ref.dtype)


def _norm(x2, nw, sc, sh, seq, tm):
    m, d = x2.shape
    per_b = seq // tm
    bvec = pl.BlockSpec((1, 1, d), lambda i: (i // per_b, 0, 0))
    return pl.pallas_call(
        _norm_kernel,
        grid=(m // tm,),
        in_specs=[pl.BlockSpec((tm, d), lambda i: (i, 0)), pl.BlockSpec((1, d), lambda i: (0, 0)), bvec, bvec],
        out_specs=pl.BlockSpec((tm, d), lambda i: (i, 0)),
        out_shape=jax.ShapeDtypeStruct((m, d), BF16),
        compiler_params=_params(("parallel",)),
    )(x2, nw, sc, sh)


def _mm_shift_kernel(a_ref, w_ref, mu_ref, o_ref, carry_ref, *, per_b):
    i = pl.program_id(0)
    j = pl.program_id(1)
    tm = a_ref.shape[0]

    @pl.when((i == 0) & (j == 0))
    def _():
        carry_ref[...] = jnp.zeros_like(carry_ref)

    before = jnp.where(i % per_b == 0, 0.0, carry_ref[j])
    w = w_ref[...]
    mu = mu_ref[...]
    slab = tm // SHIFT_SLABS
    first8 = lax.broadcasted_iota(jnp.int32, (8, w.shape[1]), 0) == 0
    outs = []
    for s in range(SHIFT_SLABS):
        z = _dot(a_ref[s * slab:(s + 1) * slab, :], w)
        rolled = pltpu.roll(z, 1, axis=0)
        prev = jnp.concatenate([jnp.where(first8, before, rolled[:8]), rolled[8:]], axis=0)
        outs.append((z + (prev - z) * mu).astype(o_ref.dtype))
        before = z[slab - 1:slab, :]
    carry_ref[j] = before
    o_ref[...] = jnp.concatenate(outs, axis=0)


def _mm_shift(a, w, mu, out_dtype, seq, tm, tn):
    m, k = a.shape
    n = w.shape[1]
    return pl.pallas_call(
        functools.partial(_mm_shift_kernel, per_b=seq // tm),
        grid=(m // tm, n // tn),
        in_specs=[pl.BlockSpec((tm, k), lambda i, j: (i, 0)), pl.BlockSpec((k, tn), lambda i, j: (0, j)),
                  pl.BlockSpec((1, tn), lambda i, j: (0, j))],
        out_specs=pl.BlockSpec((tm, tn), lambda i, j: (i, j)),
        out_shape=jax.ShapeDtypeStruct((m, n), out_dtype),
        scratch_shapes=[pltpu.VMEM((n // tn, 1, tn), F32)],
        compiler_params=_params(("arbitrary", "arbitrary")),
    )(a, w, mu)


MAX_ROW_STRIDE = 4


def _mm_dilate_kernel(a_ref, w_ref, cs_ref, o_ref, acc_ref, tmp_ref, *, dilation):
    tm = a_ref.shape[0]
    rows = tm // dilation
    acc = _dot(a_ref[...], w_ref[...]) * cs_ref[...]
    if dilation == 1:
        o_ref[0, 0] = acc.astype(o_ref.dtype)
        return
    nct = acc.shape[1] // LANES
    for c in range(nct):
        acc_ref[c] = acc[:, c * LANES:(c + 1) * LANES]

    def emit(r, src_ref, start, stride):
        o_ref[0, r] = jnp.concatenate(
            [src_ref[c, pl.ds(start, rows, stride=stride), :] for c in range(nct)], axis=-1).astype(o_ref.dtype)

    if dilation <= MAX_ROW_STRIDE:
        for r in range(dilation):
            emit(r, acc_ref, r, dilation)
        return
    s1 = MAX_ROW_STRIDE
    s2 = dilation // s1
    part = tm // s1
    for q in range(s1):
        for c in range(nct):
            tmp_ref[c, q * part:(q + 1) * part, :] = acc_ref[c, pl.ds(q, part, stride=s1), :]
    for q in range(s1):
        for q2 in range(s2):
            emit(q + s1 * q2, tmp_ref, q * part + q2, s2)


def _mm_dilate(a, w, col0, col_scale, dilation, batch, seq, tm, tn):
    m, k = a.shape
    n = col_scale.shape[1]
    cb = col0 // tn
    per_b = seq // tm
    rows = tm // dilation
    return pl.pallas_call(
        functools.partial(_mm_dilate_kernel, dilation=dilation),
        grid=(m // tm, n // tn),
        in_specs=[pl.BlockSpec((tm, k), lambda i, j: (i, 0)), pl.BlockSpec((k, tn), lambda i, j: (0, cb + j)),
                  pl.BlockSpec((1, tn), lambda i, j: (0, j))],
        out_specs=pl.BlockSpec((1, dilation, rows, tn), lambda i, j: (i // per_b, 0, i % per_b, j)),
        out_shape=jax.ShapeDtypeStruct((batch, dilation, seq // dilation, n), BF16),
        scratch_shapes=[pltpu.VMEM((tn // LANES, tm, LANES), F32)] * 2,
        compiler_params=_params(("parallel", "arbitrary")),
    )(a, w, col_scale)


def _att_kernel(q_ref, k_ref, kh_ref, v_ref, vh_ref, o_ref, l_ref, *, blocks_per_seq):
    i = pl.program_id(0)
    nq = q_ref.shape[0] // ATT_BLOCK
    qi = lax.broadcasted_iota(jnp.int32, (ATT_BLOCK, 2 * ATT_BLOCK), 0)
    kj = lax.broadcasted_iota(jnp.int32, (ATT_BLOCK, 2 * ATT_BLOCK), 1)
    band = (kj >= qi) & (kj <= qi + ATT_BLOCK)
    neg = jnp.float32(-1e30)
    bias_std = jnp.where(band, 0.0, neg)
    bias_first = jnp.where(band & (kj >= ATT_BLOCK), 0.0, neg)
    lane = lax.broadcasted_iota(jnp.int32, (ATT_BLOCK, LANES), 1)
    o_blocks, l_blocks = [], []
    for qb in range(nq):
        rows = slice(qb * ATT_BLOCK, (qb + 1) * ATT_BLOCK)
        prev = slice((qb - 1) * ATT_BLOCK, qb * ATT_BLOCK)
        first = (i * nq + qb) % blocks_per_seq == 0
        bias = jnp.where(first, bias_first, bias_std)
        q = q_ref[rows, :]
        k = jnp.concatenate([kh_ref[...] if qb == 0 else k_ref[prev, :], k_ref[rows, :]], axis=0)
        v = jnp.concatenate([vh_ref[...] if qb == 0 else v_ref[prev, :], v_ref[rows, :]], axis=0)
        outs = []
        lse_tile = jnp.zeros((ATT_BLOCK, LANES), F32)
        for hp in range(ATT_HEADS // 2):
            tile = slice(hp * LANES, (hp + 1) * LANES)
            q2, k2, v2 = q[:, tile], k[:, tile], v[:, tile]
            halves = []
            for side in range(2):
                own = (lane < HEAD_DIM) if side == 0 else (lane >= HEAD_DIM)
                s = _dot_nt(jnp.where(own, q2, jnp.zeros((), q2.dtype)), k2) + bias
                m = jnp.max(s, axis=-1, keepdims=True)
                p = jnp.exp(s - m)
                den = jnp.sum(p, axis=-1, keepdims=True)
                halves.append(_dot(p.astype(BF16), v2) / den)
                lse_tile = jnp.where(lane == 2 * hp + side, m + jnp.log(den), lse_tile)
            outs.append(jnp.where(lane < HEAD_DIM, halves[0], halves[1]))
        o_blocks.append(jnp.concatenate(outs, axis=-1).astype(o_ref.dtype))
        l_blocks.append(lse_tile)
    o_ref[...] = jnp.concatenate(o_blocks, axis=0)
    l_ref[...] = jnp.concatenate(l_blocks, axis=0)


def _attention(qkv, blocks_per_seq):
    rows = qkv.shape[0]
    tq = ATT_TQ
    hb = tq // ATT_BLOCK
    main = lambda c: pl.BlockSpec((tq, ATT_WIDTH), lambda i: (i, c))
    halo = lambda c: pl.BlockSpec((ATT_BLOCK, ATT_WIDTH), lambda i: (jnp.maximum(i * hb - 1, 0), c))
    return pl.pallas_call(
        functools.partial(_att_kernel, blocks_per_seq=blocks_per_seq),
        grid=(rows // tq,),
        in_specs=[main(0), main(1), halo(1), main(2), halo(2)],
        out_specs=[pl.BlockSpec((tq, ATT_WIDTH), lambda i: (i, 0)), pl.BlockSpec((tq, LANES), lambda i: (i, 0))],
        out_shape=[jax.ShapeDtypeStruct((rows, ATT_WIDTH), BF16), jax.ShapeDtypeStruct((rows, LANES), F32)],
        compiler_params=_params(("parallel",)),
    )(qkv, qkv, qkv, qkv, qkv)


def _rwkv_kernel(zr_ref, zk_ref, zv_ref, zl_ref, w0_ref, w2_ref, a0_ref, a2_ref, g2_ref, kk_ref, ka_ref, rk_ref,
                 tri_ref, hsum_ref, hexp_ref, lnw_ref, lnb_ref, o_ref, state_ref):
    C = CHUNK
    N = HEAD_DIM

    @pl.when(pl.program_id(1) == 0)
    def _():
        state_ref[...] = jnp.zeros_like(state_ref)

    r = zr_ref[...].astype(F32)
    k = zk_ref[...].astype(F32)
    v = zv_ref[...].astype(F32)
    zl = zl_ref[...].astype(F32)
    w_low = zl[:, :DECAY_LORA]
    a_low = zl[:, DECAY_LORA:DECAY_LORA + AAA_LORA]
    g_low = zl[:, DECAY_LORA + AAA_LORA:DECAY_LORA + AAA_LORA + GATE_LORA]

    wpre = w0_ref[...] + _dot(jnp.tanh(w_low).astype(BF16), w2_ref[...])
    lw = jax.nn.sigmoid(wpre) * (-DECAY_SCALE)
    a = jax.nn.sigmoid(a0_ref[...] + _dot(a_low.astype(BF16), a2_ref[...]))
    g = _dot(jax.nn.sigmoid(g_low).astype(BF16), g2_ref[...])
    km = k * (1.0 + (a - 1.0) * ka_ref[...])

    kn = k * kk_ref[...]
    inv = lax.rsqrt(jnp.maximum(_dot2(kn * kn, hsum_ref[...]), 1e-24))
    kn = kn * _dot2(inv, hexp_ref[...])
    rk_sum = _dot((r * km * rk_ref[...]).astype(BF16), hsum_ref[...])
    bonus = _dot(rk_sum.astype(BF16), hexp_ref[...]) * v
    vb = v.astype(BF16)

    P2 = 2 * N
    lane = lambda shape: lax.broadcasted_iota(jnp.int32, shape, 1)
    row = lambda shape: lax.broadcasted_iota(jnp.int32, shape, 0)
    lo_c = lane((C, P2)) < N
    bd = (row((P2, P2)) < N) == (lane((P2, P2)) < N)
    ri = row((2 * C, 4 * C))
    cj = lane((2 * C, 4 * C)) & (C - 1)
    keep = cj < (ri & (C - 1)) + jnp.where(ri >= C, 1, 0)
    eye2 = ((lane((C, P2)) & (N - 1)) == row((C, P2))).astype(F32)
    zero_b = jnp.zeros((), BF16)
    odd_head = (lane((C, r.shape[1])) & N) != 0
    l1 = lw.astype(BF16)
    l2 = (lw - l1.astype(F32)).astype(BF16)
    cum_all = _dot(tri_ref[...], l1) + _dot(tri_ref[...], l2)

    def blockdiag(m):
        return jnp.where(bd, jnp.concatenate([m, m], axis=0), jnp.zeros((), m.dtype))

    n_pairs = RWKV_HEADS // 2
    group = [range(n_pairs)]
    each = lambda f: {p: f(p) for p in group[0]}
    tl = lambda p: slice(p * P2, (p + 1) * P2)

    def chunk_dense(ci):
        rows = slice(ci * C, (ci + 1) * C)
        lwc = lw[rows]
        cum = cum_all[rows]
        e_pos = jnp.exp(cum)
        e_neg = jnp.exp(-cum)
        knc = kn[rows]
        xa = (-knc * jnp.exp(cum - lwc)).astype(BF16)
        xr = (r[rows] * e_pos).astype(BF16)
        zk = km[rows] * e_neg
        zb = knc * a[rows] * e_neg
        zt = jnp.concatenate([jnp.where(odd_head, zk, zb), jnp.where(odd_head, zb, zk)], axis=0).T.astype(BF16)
        return xa, xr, zt, e_pos[C - 1:C, :]

    def state_free(ci, dense):
        rows = slice(ci * C, (ci + 1) * C)
        xa, xr, zt, glast = dense
        X = each(lambda p: jnp.concatenate([xa[:, tl(p)], xr[:, tl(p)]], axis=0))
        ZT = each(lambda p: zt[tl(p), :])
        Vb = each(lambda p: vb[rows, tl(p)])
        A = each(lambda p: jnp.where(keep, _dot(X[p], jnp.concatenate(
            [jnp.where(bd, ZT[p], zero_b), jnp.where(bd, zero_b, ZT[p])], axis=1)), 0.0))
        Lb = each(lambda p: A[p][:C, :P2].astype(BF16))
        Pm = each(lambda p: _dot(Lb[p], blockdiag(Lb[p])))
        T = each(lambda p: eye2 + A[p][:C, :P2])
        span = 2
        while 2 * span < C:
            R = each(lambda p: _dot(jnp.concatenate([Pm[p], T[p]], axis=0).astype(BF16),
                                    blockdiag(Pm[p].astype(BF16))))
            Pm = each(lambda p: R[p][:C])
            T = each(lambda p: T[p] + R[p][C:])
            span *= 2
        Tb = each(lambda p: (T[p] + _dot(T[p].astype(BF16), blockdiag(Pm[p].astype(BF16)))).astype(BF16))
        XA = each(lambda p: jnp.concatenate([X[p], A[p][:, P2:].astype(BF16)], axis=1))
        Vad = each(lambda p: jnp.where(bd, zero_b, jnp.concatenate([Vb[p], Vb[p]], axis=0)))
        Arb = each(lambda p: A[p][C:, :P2].astype(BF16))
        gcol = each(lambda p: jnp.broadcast_to(glast[:, tl(p)], (P2, P2)).T)
        return XA, Vad, ZT, Vb, Tb, Arb, gcol

    def advance(pre, H_all):
        XA, Vad, ZT, Vb, Tb, Arb, gcol = pre
        XS = each(lambda p: _dot(XA[p], jnp.concatenate([H_all[p].astype(BF16), Vad[p]], axis=0)))
        W = each(lambda p: XS[p][:C].astype(BF16))
        U = each(lambda p: _dot(Tb[p], blockdiag(W[p])))
        Ub = each(lambda p: U[p].astype(BF16))
        Y = each(lambda p: XS[p][C:] + _dot(Arb[p], blockdiag(Ub[p])))
        UV = each(lambda p: jnp.concatenate([jnp.where(lo_c, Ub[p], Vb[p]), jnp.where(lo_c, Vb[p], Ub[p])], axis=0))
        H_new = each(lambda p: jnp.where(bd, H_all[p] + _dot(ZT[p], UV[p]), 0.0) * gcol[p])
        return Y, H_new

    def group_norm(y):
        def head_mean(t):
            s_lo = jnp.sum(jnp.where(lo_c, t, 0.0), axis=-1, keepdims=True)
            s_hi = jnp.sum(jnp.where(lo_c, 0.0, t), axis=-1, keepdims=True)
            return jnp.where(lo_c, s_lo, s_hi) * (1.0 / N)
        dlt = y - head_mean(y)
        return dlt * lax.rsqrt(head_mean(dlt * dlt) + GN_EPS)

    n_chunks = zr_ref.shape[0] // C
    dense = [chunk_dense(ci) for ci in range(n_chunks)]
    H_out = {}
    y_cols = []
    for g0 in range(0, n_pairs, PAIRS_PER_GROUP):
        group[0] = range(g0, g0 + PAIRS_PER_GROUP)
        pres = [state_free(ci, dense[ci]) for ci in range(n_chunks)]
        H_all = each(lambda p: state_ref[p])
        ys = []
        for ci in range(n_chunks):
            Y, H_all = advance(pres[ci], H_all)
            ys.append(jnp.concatenate([group_norm(Y[p]) for p in group[0]], axis=-1))
        H_out.update(H_all)
        y_cols.append(jnp.concatenate(ys, axis=0))
    state_ref[...] = jnp.stack([H_out[p] for p in range(n_pairs)])
    y = jnp.concatenate(y_cols, axis=-1) * lnw_ref[...] + lnb_ref[...]
    o_ref[...] = ((y + bonus) * g).astype(o_ref.dtype)


def _rwkv(zall, col0, w0, w2, a0, a2, g2, k_k, k_a, r_k, lnx_w, lnx_b, batch, seq):
    m = zall.shape[0]
    d = w0.shape[1]
    tm = CHUNK * CHUNKS_PER_STEP
    per_b = seq // tm
    cb = col0 // d
    row = lambda b, t: (b * per_b + t, 0)
    fixed = lambda b, t: (0, 0)
    zspec = lambda c: pl.BlockSpec((tm, d), lambda b, t: (b * per_b + t, cb + c))
    lspec = pl.BlockSpec((tm, LORA_PAD), lambda b, t: (b * per_b + t, (col0 + 3 * d) // LORA_PAD))
    vec = pl.BlockSpec((1, d), fixed)
    t_idx = jnp.arange(tm)
    tri = ((t_idx[:, None] >= t_idx[None, :]) &
           (t_idx[:, None] // CHUNK == t_idx[None, :] // CHUNK)).astype(BF16)
    head_of = jnp.arange(d) // HEAD_DIM
    hsum = (head_of[:, None] == jnp.arange(LANES)[None, :]).astype(BF16)
    hexp = hsum.T
    return pl.pallas_call(
        _rwkv_kernel,
        grid=(batch, per_b),
        in_specs=[zspec(0), zspec(1), zspec(2), lspec,
                  vec, pl.BlockSpec((DECAY_LORA, d), fixed),
                  vec, pl.BlockSpec((AAA_LORA, d), fixed),
                  pl.BlockSpec((GATE_LORA, d), fixed), vec, vec, vec,
                  pl.BlockSpec((tm, tm), fixed), pl.BlockSpec((d, LANES), fixed),
                  pl.BlockSpec((LANES, d), fixed), vec, vec],
        out_specs=pl.BlockSpec((tm, d), row),
        out_shape=jax.ShapeDtypeStruct((m, d), BF16),
        scratch_shapes=[pltpu.VMEM((RWKV_HEADS // 2, 2 * HEAD_DIM, 2 * HEAD_DIM), F32)],
        compiler_params=_params(("parallel", "arbitrary")),
    )(zall, zall, zall, zall, w0, w2, a0, a2, g2, k_k, k_a, r_k, tri, hsum, hexp, lnx_w, lnx_b)


def _mix_kernel(o1, o2, o3, l1, l2, l3, rw_ref, gate_ref, bg_ref, x_ref, gt_ref, hexp_ref, wa_ref, wr_ref, wo_ref,
                out_ref, os_ref, ls_ref):
    tm = x_ref.shape[0]

    def token_order(o_ref, l_ref):
        d = o_ref.shape[1]
        if d == 1:
            return o_ref[0, 0].astype(F32), l_ref[0, 0]
        rows = tm // d
        nct = os_ref.shape[0]
        for r in range(d):
            o = o_ref[0, r].astype(F32)
            for c in range(nct):
                os_ref[c, pl.ds(r, rows, stride=d), :] = o[:, c * LANES:(c + 1) * LANES]
            ls_ref[pl.ds(r, rows, stride=d), :] = l_ref[0, r]
        return jnp.concatenate([os_ref[c] for c in range(nct)], axis=-1), ls_ref[...]

    oa, la = token_order(o1, l1)
    ob, lb = token_order(o2, l2)
    oc, lc = token_order(o3, l3)
    mx = jnp.maximum(jnp.maximum(la, lb), lc)
    ea, eb, ec = jnp.exp(la - mx), jnp.exp(lb - mx), jnp.exp(lc - mx)
    inv = 1.0 / (ea + eb + ec)
    hexp = hexp_ref[...]
    att = (_dot2(ea * inv, hexp) * oa + _dot2(eb * inv, hexp) * ob + _dot2(ec * inv, hexp) * oc)
    y_att = _dot(att.astype(BF16), wa_ref[...])
    y_rwkv = _dot(rw_ref[...], wr_ref[...])
    gates = jax.nn.sigmoid(gate_ref[...].astype(F32) + bg_ref[...])
    d = y_att.shape[1]
    mix = gates[:, :d] * y_att + gates[:, d:] * y_rwkv
    out_ref[...] = x_ref[...] + gt_ref[0] * _dot(mix.astype(BF16), wo_ref[...])


def _mix(att_o, att_l, rw, zall, b_gate, x2, gt1, wa, wr, wo, batch, seq, tm):
    m, d = x2.shape
    per_b = seq // tm
    row = lambda w: pl.BlockSpec((tm, w), lambda i: (i, 0))
    full = lambda a: pl.BlockSpec(a.shape, lambda i: (0, 0))

    def dil(arr):
        dd = arr.shape[1]
        return pl.BlockSpec((1, dd, tm // dd, arr.shape[3]), lambda i: (i // per_b, 0, i % per_b, 0))

    hexp = (jnp.arange(LANES)[:, None] == (jnp.arange(ATT_WIDTH) // HEAD_DIM)[None, :]).astype(BF16)
    return pl.pallas_call(
        _mix_kernel,
        grid=(m // tm,),
        in_specs=[dil(a) for a in att_o] + [dil(a) for a in att_l] +
                 [row(d), row(2 * d), full(b_gate), row(d),
                  pl.BlockSpec((1, 1, d), lambda i: (i // per_b, 0, 0)), full(hexp), full(wa), full(wr), full(wo)],
        out_specs=row(d),
        out_shape=jax.ShapeDtypeStruct((m, d), F32),
        scratch_shapes=[pltpu.VMEM((ATT_WIDTH // LANES, tm, LANES), F32), pltpu.VMEM((tm, LANES), F32)],
        compiler_params=_params(("parallel",)),
    )(*att_o, *att_l, rw, zall, b_gate, x2, gt1, hexp, wa, wr, wo)


HALO = 16


def _ffn_kernel(x_ref, xh_ref, nw_ref, sc_ref, sh_ref, gt_ref, wu_ref, cw_ref, cb_ref, wd_ref, nf_ref, o_ref,
                *, per_b, tf):
    i = pl.program_id(0)
    tm = x_ref.shape[0]
    f = wd_ref.shape[0]
    x = x_ref[...]
    hh = _adaln(xh_ref[...], nw_ref[...], sc_ref[0], sh_ref[0])
    h = jnp.concatenate([jnp.where(i % per_b == 0, 0.0, hh).astype(BF16),
                         _adaln(x, nw_ref[...], sc_ref[0], sh_ref[0]).astype(BF16)], axis=0)

    def conv(u, cols):
        out = cb_ref[:, cols] + cw_ref[2:3, cols] * u[HALO:, :]
        out = out + cw_ref[1:2, cols] * u[HALO - 1:HALO - 1 + tm, :]
        return out + cw_ref[0:1, cols] * u[HALO - 2:HALO - 2 + tm, :]

    acc = None
    for c0 in range(0, f, tf):
        gcols = slice(c0, c0 + tf)
        vcols = slice(f + c0, f + c0 + tf)
        gate = conv(_dot(h, wu_ref[:, gcols]), gcols)
        val = conv(_dot(h, wu_ref[:, vcols]), vcols)
        act = (gate * jax.nn.sigmoid(gate) * val).astype(BF16)
        part = _dot(act, wd_ref[gcols, :])
        acc = part if acc is None else acc + part
    x2 = x + gt_ref[0] * acc
    y = x2 * lax.rsqrt(jnp.mean(x2 * x2, axis=-1, keepdims=True) + RMS_EPS)
    o_ref[...] = y * nf_ref[...]


def _ffn(x1, nw, sc, sh, gt, w_up, conv_w, conv_b, w_down, nf, seq, tm, tf):
    m, d = x1.shape
    f = w_down.shape[0]
    per_b = seq // tm
    hb = tm // HALO
    rowc = lambda i: (i, 0)
    fixed = lambda i: (0, 0)
    bvec = pl.BlockSpec((1, 1, d), lambda i: (i // per_b, 0, 0))
    resident = lambda shape: pl.BlockSpec(shape, fixed, pipeline_mode=pl.Buffered(1))
    return pl.pallas_call(
        functools.partial(_ffn_kernel, per_b=per_b, tf=tf),
        grid=(m // tm,),
        in_specs=[pl.BlockSpec((tm, d), rowc),
                  pl.BlockSpec((HALO, d), lambda i: (jnp.maximum(i * hb - 1, 0), 0)),
                  pl.BlockSpec((1, d), fixed), bvec, bvec, bvec,
                  resident((d, 2 * f)), resident((CONV_WIDTH, 2 * f)), resident((1, 2 * f)), resident((f, d)),
                  pl.BlockSpec((1, d), fixed)],
        out_specs=pl.BlockSpec((tm, d), rowc),
        out_shape=jax.ShapeDtypeStruct((m, d), F32),
        compiler_params=pltpu.CompilerParams(dimension_semantics=("parallel",), vmem_limit_bytes=FFN_VMEM_LIMIT),
    )(x1, x1, nw, sc, sh, gt, w_up, conv_w, conv_b.reshape(1, 2 * f), w_down, nf)


def kernel(x, c, w_ada, b_ada, norm1_w, w_in, b_gate, mu_shift, w0, w2, a0, a2, g2, k_k, k_a, r_k, lnx_w, lnx_b,
           w_att_out, w_rwkv_out, w_o, norm2_w, w_up, conv_w, conv_b, w_down, norm_f_w):
    batch, seq, d = x.shape
    assert w_ada.shape[0] == 1, "the fused ffn kernel applies the final RMSNorm; one layer only"
    l = 0
    grp = 3 * ATT_WIDTH
    att_in = len(ATT_PATTERNS) * grp
    lora = DECAY_LORA + AAA_LORA + GATE_LORA
    xf = x.reshape(batch * seq, d)

    ada = _ada(c, w_ada[l], b_ada[l])
    sh1, sc1, gt1, sh2, sc2, gt2 = [t.reshape(batch, 1, d) for t in jnp.split(ada, 6, axis=-1)]
    h = _norm(xf, norm1_w[l].reshape(1, d), sc1, sh1, seq, NORM_TM)

    win = w_in[l].astype(BF16)
    w_rest = jnp.concatenate(
        [win[:, att_in + 3 * d + lora:], win[:, att_in:att_in + 3 * d + lora],
         jnp.zeros((d, LORA_PAD - lora), BF16)], axis=1)
    mu_all = jnp.concatenate([jnp.zeros((2 * d,), F32), mu_shift[l], jnp.zeros((LORA_PAD - lora,), F32)])
    zall = _mm_shift(h, w_rest, mu_all.reshape(1, -1), BF16, seq, ZALL_TM, ZALL_TN)

    att_o, att_l = [], []
    qkv_scale = jnp.concatenate([jnp.full((ATT_WIDTH,), HEAD_DIM ** -0.5, F32),
                                 jnp.ones((2 * ATT_WIDTH,), F32)]).reshape(1, grp)
    for gi, (_, dilation) in enumerate(ATT_PATTERNS):
        qkv = _mm_dilate(h, win, gi * grp, qkv_scale, dilation, batch, seq, QKV_TM, QKV_TN)
        o, lse = _attention(qkv.reshape(batch * seq, grp), seq // dilation // ATT_BLOCK)
        att_o.append(o.reshape(batch, dilation, seq // dilation, ATT_WIDTH))
        att_l.append(lse.reshape(batch, dilation, seq // dilation, LANES))

    vec = lambda t: t.reshape(1, d)
    rw = _rwkv(zall, 2 * d, vec(w0[l]), w2[l].astype(BF16), vec(a0[l]), a2[l].astype(BF16), g2[l].astype(BF16),
               vec(k_k[l]), vec(k_a[l]), vec(r_k[l]), vec(lnx_w[l]), vec(lnx_b[l]), batch, seq)

    x1 = _mix(att_o, att_l, rw, zall, b_gate[l].reshape(1, 2 * d), xf, gt1,
              w_att_out[l].astype(BF16), w_rwkv_out[l].astype(BF16), w_o[l].astype(BF16), batch, seq, MIX_TM)
    out = _ffn(x1, norm2_w[l].reshape(1, d), sc2, sh2, gt2, w_up[l].astype(BF16), conv_w[l], conv_b[l],
               w_down[l].astype(BF16), norm_f_w.reshape(1, d), seq, FFN_TM, FFN_TF)
    return out.reshape(batch, seq, d)
```

```python
import functools
import math

import jax
import jax.numpy as jnp
from jax import lax
from jax.experimental import pallas as pl
from jax.experimental.pallas import tpu as pltpu

F32 = jnp.float32
BF16 = jnp.bfloat16

LANES = 128
ATT_PATTERNS = ((128, 1), (512, 4), (2048, 16))
ATT_HEADS = 8
HEAD_DIM = 64
ATT_WIDTH = ATT_HEADS * HEAD_DIM
ATT_BLOCK = 128
ATT_TQ = 1024
RWKV_HEADS = 16
DECAY_LORA = 64
AAA_LORA = 64
GATE_LORA = 160
LORA_PAD = 512
SHIFT_SLABS = 4
ADA_TN = 1536
NORM_TM = 2048
ZALL_TM, ZALL_TN = 1024, 2816
QKV_TM, QKV_TN = 1024, 3 * ATT_WIDTH
MIX_TM = 512
FFN_TM, FFN_TF = 512, 2816
CONV_WIDTH = 3
RMS_EPS = 1e-6
GN_EPS = 64e-5
DECAY_SCALE = math.exp(-0.5)
CHUNK = 64
CHUNKS_PER_STEP = 4
VMEM_LIMIT = 48 * 1024 * 1024
FFN_VMEM_LIMIT = 56 * 1024 * 1024


def _dot(a, b):
    return jnp.dot(a, b, preferred_element_type=F32)


def _dot_nt(a, b):
    return lax.dot_general(a, b, (((1,), (1,)), ((), ())), preferred_element_type=F32)


def _split3(x):
    hi = x.astype(BF16)
    r1 = x - hi.astype(F32)
    mid = r1.astype(BF16)
    lo = (r1 - mid.astype(F32)).astype(BF16)
    return hi, mid, lo


def _dot2(x, w):
    hi = x.astype(BF16)
    lo = (x - hi.astype(F32)).astype(BF16)
    return _dot(hi, w) + _dot(lo, w)


def _params(sem):
    return pltpu.CompilerParams(dimension_semantics=sem, vmem_limit_bytes=VMEM_LIMIT)


def _ada_kernel(c_ref, w_ref, b_ref, o_ref):
    ch, cl, _ = _split3(c_ref[...])
    wh, wl, _ = _split3(w_ref[...])
    o_ref[...] = _dot(ch, wh) + _dot(cl, wh) + _dot(ch, wl) + b_ref[...]


def _ada(c, w_ada, b_ada):
    b, d = c.shape
    n = w_ada.shape[1]
    tn = ADA_TN
    return pl.pallas_call(
        _ada_kernel,
        grid=(n // tn,),
        in_specs=[pl.BlockSpec((b, d), lambda j: (0, 0)),
                  pl.BlockSpec((d, tn), lambda j: (0, j)),
                  pl.BlockSpec((1, tn), lambda j: (0, j))],
        out_specs=pl.BlockSpec((b, tn), lambda j: (0, j)),
        out_shape=jax.ShapeDtypeStruct((b, n), F32),
        compiler_params=_params(("arbitrary",)),
    )(c, w_ada, b_ada.reshape(1, n))


def _adaln(x, nw, sc, sh):
    y = x * lax.rsqrt(jnp.mean(x * x, axis=-1, keepdims=True) + RMS_EPS)
    return (y * nw) * (1.0 + sc) + sh


def _norm_kernel(x_ref, nw_ref, sc_ref, sh_ref, o_ref):
    o_ref[...] = _adaln(x_ref[...], nw_ref[...], sc_ref[0], sh_ref[0]).astype(o_ref.dtype)


def _norm(x2, nw, sc, sh, seq, tm):
    m, d = x2.shape
    per_b = seq // tm
    bvec = pl.BlockSpec((1, 1, d), lambda i: (i // per_b, 0, 0))
    return pl.pallas_call(
        _norm_kernel,
        grid=(m // tm,),
        in_specs=[pl.BlockSpec((tm, d), lambda i: (i, 0)), pl.BlockSpec((1, d), lambda i: (0, 0)), bvec, bvec],
        out_specs=pl.BlockSpec((tm, d), lambda i: (i, 0)),
        out_shape=jax.ShapeDtypeStruct((m, d), BF16),
        compiler_params=_params(("parallel",)),
    )(x2, nw, sc, sh)


def _mm_shift_kernel(a_ref, w_ref, mu_ref, o_ref, carry_ref, *, per_b):
    i = pl.program_id(0)
    j = pl.program_id(1)
    tm = a_ref.shape[0]

    @pl.when((i == 0) & (j == 0))
    def _():
        carry_ref[...] = jnp.zeros_like(carry_ref)

    before = jnp.where(i % per_b == 0, 0.0, carry_ref[j])
    w = w_ref[...]
    mu = mu_ref[...]
    slab = tm // SHIFT_SLABS
    first8 = lax.broadcasted_iota(jnp.int32, (8, w.shape[1]), 0) == 0
    outs = []
    for s in range(SHIFT_SLABS):
        z = _dot(a_ref[s * slab:(s + 1) * slab, :], w)
        rolled = pltpu.roll(z, 1, axis=0)
        prev = jnp.concatenate([jnp.where(first8, before, rolled[:8]), rolled[8:]], axis=0)
        outs.append((z + (prev - z) * mu).astype(o_ref.dtype))
        before = z[slab - 1:slab, :]
    carry_ref[j] = before
    o_ref[...] = jnp.concatenate(outs, axis=0)


def _mm_shift(a, w, mu, out_dtype, seq, tm, tn):
    m, k = a.shape
    n = w.shape[1]
    return pl.pallas_call(
        functools.partial(_mm_shift_kernel, per_b=seq // tm),
        grid=(m // tm, n // tn),
        in_specs=[pl.BlockSpec((tm, k), lambda i, j: (i, 0)), pl.BlockSpec((k, tn), lambda i, j: (0, j)),
                  pl.BlockSpec((1, tn), lambda i, j: (0, j))],
        out_specs=pl.BlockSpec((tm, tn), lambda i, j: (i, j)),
        out_shape=jax.ShapeDtypeStruct((m, n), out_dtype),
        scratch_shapes=[pltpu.VMEM((n // tn, 1, tn), F32)],
        compiler_params=_params(("arbitrary", "arbitrary")),
    )(a, w, mu)


MAX_ROW_STRIDE = 4


def _mm_dilate_kernel(a_ref, w_ref, cs_ref, o_ref, acc_ref, tmp_ref, *, dilation):
    tm = a_ref.shape[0]
    rows = tm // dilation
    acc = _dot(a_ref[...], w_ref[...]) * cs_ref[...]
    if dilation == 1:
        o_ref[0, 0] = acc.astype(o_ref.dtype)
        return
    nct = acc.shape[1] // LANES
    for c in range(nct):
        acc_ref[c] = acc[:, c * LANES:(c + 1) * LANES]

    def emit(r, src_ref, start, stride):
        o_ref[0, r] = jnp.concatenate(
            [src_ref[c, pl.ds(start, rows, stride=stride), :] for c in range(nct)], axis=-1).astype(o_ref.dtype)

    if dilation <= MAX_ROW_STRIDE:
        for r in range(dilation):
            emit(r, acc_ref, r, dilation)
        return
    s1 = MAX_ROW_STRIDE
    s2 = dilation // s1
    part = tm // s1
    for q in range(s1):
        for c in range(nct):
            tmp_ref[c, q * part:(q + 1) * part, :] = acc_ref[c, pl.ds(q, part, stride=s1), :]
    for q in range(s1):
        for q2 in range(s2):
            emit(q + s1 * q2, tmp_ref, q * part + q2, s2)


def _mm_dilate(a, w, col0, col_scale, dilation, batch, seq, tm, tn):
    m, k = a.shape
    n = col_scale.shape[1]
    cb = col0 // tn
    per_b = seq // tm
    rows = tm // dilation
    return pl.pallas_call(
        functools.partial(_mm_dilate_kernel, dilation=dilation),
        grid=(m // tm, n // tn),
        in_specs=[pl.BlockSpec((tm, k), lambda i, j: (i, 0)), pl.BlockSpec((k, tn), lambda i, j: (0, cb + j)),
                  pl.BlockSpec((1, tn), lambda i, j: (0, j))],
        out_specs=pl.BlockSpec((1, dilation, rows, tn), lambda i, j: (i // per_b, 0, i % per_b, j)),
        out_shape=jax.ShapeDtypeStruct((batch, dilation, seq // dilation, n), BF16),
        scratch_shapes=[pltpu.VMEM((tn // LANES, tm, LANES), F32)] * 2,
        compiler_params=_params(("parallel", "arbitrary")),
    )(a, w, col_scale)


def _att_kernel(q_ref, k_ref, kh_ref, v_ref, vh_ref, o_ref, l_ref, *, blocks_per_seq):
    i = pl.program_id(0)
    nq = q_ref.shape[0] // ATT_BLOCK
    qi = lax.broadcasted_iota(jnp.int32, (ATT_BLOCK, 2 * ATT_BLOCK), 0)
    kj = lax.broadcasted_iota(jnp.int32, (ATT_BLOCK, 2 * ATT_BLOCK), 1)
    band = (kj >= qi) & (kj <= qi + ATT_BLOCK)
    neg = jnp.float32(-1e30)
    bias_std = jnp.where(band, 0.0, neg)
    bias_first = jnp.where(band & (kj >= ATT_BLOCK), 0.0, neg)
    lane = lax.broadcasted_iota(jnp.int32, (ATT_BLOCK, LANES), 1)
    o_blocks, l_blocks = [], []
    for qb in range(nq):
        rows = slice(qb * ATT_BLOCK, (qb + 1) * ATT_BLOCK)
        prev = slice((qb - 1) * ATT_BLOCK, qb * ATT_BLOCK)
        first = (i * nq + qb) % blocks_per_seq == 0
        bias = jnp.where(first, bias_first, bias_std)
        q = q_ref[rows, :]
        k = jnp.concatenate([kh_ref[...] if qb == 0 else k_ref[prev, :], k_ref[rows, :]], axis=0)
        v = jnp.concatenate([vh_ref[...] if qb == 0 else v_ref[prev, :], v_ref[rows, :]], axis=0)
        outs = []
        lse_tile = jnp.zeros((ATT_BLOCK, LANES), F32)
        for hp in range(ATT_HEADS // 2):
            tile = slice(hp * LANES, (hp + 1) * LANES)
            q2, k2, v2 = q[:, tile], k[:, tile], v[:, tile]
            halves = []
            for side in range(2):
                own = (lane < HEAD_DIM) if side == 0 else (lane >= HEAD_DIM)
                s = _dot_nt(jnp.where(own, q2, jnp.zeros((), q2.dtype)), k2) + bias
                m = jnp.max(s, axis=-1, keepdims=True)
                p = jnp.exp(s - m)
                den = jnp.sum(p, axis=-1, keepdims=True)
                halves.append(_dot(p.astype(BF16), v2) / den)
                lse_tile = jnp.where(lane == 2 * hp + side, m + jnp.log(den), lse_tile)
            outs.append(jnp.where(lane < HEAD_DIM, halves[0], halves[1]))
        o_blocks.append(jnp.concatenate(outs, axis=-1).astype(o_ref.dtype))
        l_blocks.append(lse_tile)
    o_ref[...] = jnp.concatenate(o_blocks, axis=0)
    l_ref[...] = jnp.concatenate(l_blocks, axis=0)


def _attention(qkv, blocks_per_seq):
    rows = qkv.shape[0]
    tq = ATT_TQ
    hb = tq // ATT_BLOCK
    main = lambda c: pl.BlockSpec((tq, ATT_WIDTH), lambda i: (i, c))
    halo = lambda c: pl.BlockSpec((ATT_BLOCK, ATT_WIDTH), lambda i: (jnp.maximum(i * hb - 1, 0), c))
    return pl.pallas_call(
        functools.partial(_att_kernel, blocks_per_seq=blocks_per_seq),
        grid=(rows // tq,),
        in_specs=[main(0), main(1), halo(1), main(2), halo(2)],
        out_specs=[pl.BlockSpec((tq, ATT_WIDTH), lambda i: (i, 0)), pl.BlockSpec((tq, LANES), lambda i: (i, 0))],
        out_shape=[jax.ShapeDtypeStruct((rows, ATT_WIDTH), BF16), jax.ShapeDtypeStruct((rows, LANES), F32)],
        compiler_params=_params(("parallel",)),
    )(qkv, qkv, qkv, qkv, qkv)


def _rwkv_kernel(zr_ref, zk_ref, zv_ref, zl_ref, w0_ref, w2_ref, a0_ref, a2_ref, g2_ref, kk_ref, ka_ref, rk_ref,
                 tri_ref, hsum_ref, hexp_ref, lnw_ref, lnb_ref, o_ref, state_ref):
    C = CHUNK
    N = HEAD_DIM

    @pl.when(pl.program_id(1) == 0)
    def _():
        state_ref[...] = jnp.zeros_like(state_ref)

    r = zr_ref[...].astype(F32)
    k = zk_ref[...].astype(F32)
    v = zv_ref[...].astype(F32)
    zl = zl_ref[...].astype(F32)
    w_low = zl[:, :DECAY_LORA]
    a_low = zl[:, DECAY_LORA:DECAY_LORA + AAA_LORA]
    g_low = zl[:, DECAY_LORA + AAA_LORA:DECAY_LORA + AAA_LORA + GATE_LORA]

    wpre = w0_ref[...] + _dot(jnp.tanh(w_low).astype(BF16), w2_ref[...])
    lw = jax.nn.sigmoid(wpre) * (-DECAY_SCALE)
    a = jax.nn.sigmoid(a0_ref[...] + _dot(a_low.astype(BF16), a2_ref[...]))
    g = _dot(jax.nn.sigmoid(g_low).astype(BF16), g2_ref[...])
    km = k * (1.0 + (a - 1.0) * ka_ref[...])

    kn = k * kk_ref[...]
    inv = lax.rsqrt(jnp.maximum(_dot2(kn * kn, hsum_ref[...]), 1e-24))
    kn = kn * _dot2(inv, hexp_ref[...])
    rk_sum = _dot((r * km * rk_ref[...]).astype(BF16), hsum_ref[...])
    bonus = _dot(rk_sum.astype(BF16), hexp_ref[...]) * v
    vb = v.astype(BF16)

    P2 = 2 * N
    lane = lambda shape: lax.broadcasted_iota(jnp.int32, shape, 1)
    row = lambda shape: lax.broadcasted_iota(jnp.int32, shape, 0)
    lo_c = lane((C, P2)) < N
    bd = (row((P2, P2)) < N) == (lane((P2, P2)) < N)
    ri = row((2 * C, 4 * C))
    cj = lane((2 * C, 4 * C)) & (C - 1)
    keep = cj < (ri & (C - 1)) + jnp.where(ri >= C, 1, 0)
    eye2 = ((lane((C, P2)) & (N - 1)) == row((C, P2))).astype(F32)
    zero_b = jnp.zeros((), BF16)
    odd_head = (lane((C, r.shape[1])) & N) != 0
    l1 = lw.astype(BF16)
    l2 = (lw - l1.astype(F32)).astype(BF16)
    cum_all = _dot(tri_ref[...], l1) + _dot(tri_ref[...], l2)

    def blockdiag(m):
        return jnp.where(bd, jnp.concatenate([m, m], axis=0), jnp.zeros((), m.dtype))

    pairs = range(RWKV_HEADS // 2)
    each = lambda f: [f(p) for p in pairs]
    tl = lambda p: slice(p * P2, (p + 1) * P2)

    def chunk_dense(ci):
        rows = slice(ci * C, (ci + 1) * C)
        lwc = lw[rows]
        cum = cum_all[rows]
        e_pos = jnp.exp(cum)
        e_neg = jnp.exp(-cum)
        knc = kn[rows]
        xa = (-knc * jnp.exp(cum - lwc)).astype(BF16)
        xr = (r[rows] * e_pos).astype(BF16)
        zk = km[rows] * e_neg
        zb = knc * a[rows] * e_neg
        zt = jnp.concatenate([jnp.where(odd_head, zk, zb), jnp.where(odd_head, zb, zk)], axis=0).T.astype(BF16)
        return xa, xr, zt, e_pos[C - 1:C, :]

    def state_free(ci, dense):
        rows = slice(ci * C, (ci + 1) * C)
        xa, xr, zt, glast = dense
        X = each(lambda p: jnp.concatenate([xa[:, tl(p)], xr[:, tl(p)]], axis=0))
        ZT = each(lambda p: zt[tl(p), :])
        Vb = each(lambda p: vb[rows, tl(p)])
        A = each(lambda p: jnp.where(keep, _dot(X[p], jnp.concatenate(
            [jnp.where(bd, ZT[p], zero_b), jnp.where(bd, zero_b, ZT[p])], axis=1)), 0.0))
        yield
        Lb = each(lambda p: A[p][:C, :P2].astype(BF16))
        Pm = each(lambda p: _dot(Lb[p], blockdiag(Lb[p])))
        T = each(lambda p: eye2 + A[p][:C, :P2])
        yield
        span = 2
        while 2 * span < C:
            R = each(lambda p: _dot(jnp.concatenate([Pm[p], T[p]], axis=0).astype(BF16),
                                    blockdiag(Pm[p].astype(BF16))))
            Pm = each(lambda p: R[p][:C])
            T = each(lambda p: T[p] + R[p][C:])
            span *= 2
            yield
        Tb = each(lambda p: (T[p] + _dot(T[p].astype(BF16), blockdiag(Pm[p].astype(BF16)))).astype(BF16))
        yield
        XA = each(lambda p: jnp.concatenate([X[p], A[p][:, P2:].astype(BF16)], axis=1))
        Vad = each(lambda p: jnp.where(bd, zero_b, jnp.concatenate([Vb[p], Vb[p]], axis=0)))
        Arb = each(lambda p: A[p][C:, :P2].astype(BF16))
        gcol = each(lambda p: jnp.broadcast_to(glast[:, tl(p)], (P2, P2)).T)
        return XA, Vad, ZT, Vb, Tb, Arb, gcol

    def advance(pre, H_all):
        XA, Vad, ZT, Vb, Tb, Arb, gcol = pre
        XS = each(lambda p: _dot(XA[p], jnp.concatenate([H_all[p].astype(BF16), Vad[p]], axis=0)))
        yield
        W = each(lambda p: XS[p][:C].astype(BF16))
        U = each(lambda p: _dot(Tb[p], blockdiag(W[p])))
        yield
        Ub = each(lambda p: U[p].astype(BF16))
        Y = each(lambda p: XS[p][C:] + _dot(Arb[p], blockdiag(Ub[p])))
        yield
        UV = each(lambda p: jnp.concatenate([jnp.where(lo_c, Ub[p], Vb[p]), jnp.where(lo_c, Vb[p], Ub[p])], axis=0))
        H_new = each(lambda p: jnp.where(bd, H_all[p] + _dot(ZT[p], UV[p]), 0.0) * gcol[p])
        return Y, H_new

    def group_norm(y):
        def head_mean(t):
            s_lo = jnp.sum(jnp.where(lo_c, t, 0.0), axis=-1, keepdims=True)
            s_hi = jnp.sum(jnp.where(lo_c, 0.0, t), axis=-1, keepdims=True)
            return jnp.where(lo_c, s_lo, s_hi) * (1.0 / N)
        dlt = y - head_mean(y)
        return dlt * lax.rsqrt(head_mean(dlt * dlt) + GN_EPS)

    n_chunks = zr_ref.shape[0] // C
    H_all = each(lambda p: state_ref[p])
    ys = []
    def trace_together(*stagewise):
        results = [None] * len(stagewise)
        live = list(enumerate(stagewise))
        while live:
            for item in list(live):
                try:
                    next(item[1])
                except StopIteration as done:
                    results[item[0]] = done.value
                    live.remove(item)
        return results

    (nxt,) = trace_together(state_free(0, chunk_dense(0)))
    for ci in range(n_chunks):
        pre = nxt
        if ci + 1 < n_chunks:
            (Y, H_all), nxt = trace_together(advance(pre, H_all), state_free(ci + 1, chunk_dense(ci + 1)))
        else:
            ((Y, H_all),) = trace_together(advance(pre, H_all))
        ys.append(jnp.concatenate([group_norm(Y[p]) for p in pairs], axis=-1))
    state_ref[...] = jnp.stack([H_all[p] for p in pairs])
    y = jnp.concatenate(ys, axis=0) * lnw_ref[...] + lnb_ref[...]
    o_ref[...] = ((y + bonus) * g).astype(o_ref.dtype)


def _rwkv(zall, col0, w0, w2, a0, a2, g2, k_k, k_a, r_k, lnx_w, lnx_b, batch, seq):
    m = zall.shape[0]
    d = w0.shape[1]
    tm = CHUNK * CHUNKS_PER_STEP
    per_b = seq // tm
    cb = col0 // d
    row = lambda b, t: (b * per_b + t, 0)
    fixed = lambda b, t: (0, 0)
    zspec = lambda c: pl.BlockSpec((tm, d), lambda b, t: (b * per_b + t, cb + c))
    lspec = pl.BlockSpec((tm, LORA_PAD), lambda b, t: (b * per_b + t, (col0 + 3 * d) // LORA_PAD))
    vec = pl.BlockSpec((1, d), fixed)
    t_idx = jnp.arange(tm)
    tri = ((t_idx[:, None] >= t_idx[None, :]) &
           (t_idx[:, None] // CHUNK == t_idx[None, :] // CHUNK)).astype(BF16)
    head_of = jnp.arange(d) // HEAD_DIM
    hsum = (head_of[:, None] == jnp.arange(LANES)[None, :]).astype(BF16)
    hexp = hsum.T
    return pl.pallas_call(
        _rwkv_kernel,
        grid=(batch, per_b),
        in_specs=[zspec(0), zspec(1), zspec(2), lspec,
                  vec, pl.BlockSpec((DECAY_LORA, d), fixed),
                  vec, pl.BlockSpec((AAA_LORA, d), fixed),
                  pl.BlockSpec((GATE_LORA, d), fixed), vec, vec, vec,
                  pl.BlockSpec((tm, tm), fixed), pl.BlockSpec((d, LANES), fixed),
                  pl.BlockSpec((LANES, d), fixed), vec, vec],
        out_specs=pl.BlockSpec((tm, d), row),
        out_shape=jax.ShapeDtypeStruct((m, d), BF16),
        scratch_shapes=[pltpu.VMEM((RWKV_HEADS // 2, 2 * HEAD_DIM, 2 * HEAD_DIM), F32)],
        compiler_params=_params(("parallel", "arbitrary")),
    )(zall, zall, zall, zall, w0, w2, a0, a2, g2, k_k, k_a, r_k, tri, hsum, hexp, lnx_w, lnx_b)


def _mix_kernel(o1, o2, o3, l1, l2, l3, rw_ref, gate_ref, bg_ref, x_ref, gt_ref, hexp_ref, wa_ref, wr_ref, wo_ref,
                out_ref, os_ref, ls_ref):
    tm = x_ref.shape[0]

    def token_order(o_ref, l_ref):
        d = o_ref.shape[1]
        if d == 1:
            return o_ref[0, 0].astype(F32), l_ref[0, 0]
        rows = tm // d
        nct = os_ref.shape[0]
        for r in range(d):
            o = o_ref[0, r].astype(F32)
            for c in range(nct):
                os_ref[c, pl.ds(r, rows, stride=d), :] = o[:, c * LANES:(c + 1) * LANES]
            ls_ref[pl.ds(r, rows, stride=d), :] = l_ref[0, r]
        return jnp.concatenate([os_ref[c] for c in range(nct)], axis=-1), ls_ref[...]

    oa, la = token_order(o1, l1)
    ob, lb = token_order(o2, l2)
    oc, lc = token_order(o3, l3)
    mx = jnp.maximum(jnp.maximum(la, lb), lc)
    ea, eb, ec = jnp.exp(la - mx), jnp.exp(lb - mx), jnp.exp(lc - mx)
    inv = 1.0 / (ea + eb + ec)
    hexp = hexp_ref[...]
    att = (_dot2(ea * inv, hexp) * oa + _dot2(eb * inv, hexp) * ob + _dot2(ec * inv, hexp) * oc)
    y_att = _dot(att.astype(BF16), wa_ref[...])
    y_rwkv = _dot(rw_ref[...], wr_ref[...])
    gates = jax.nn.sigmoid(gate_ref[...].astype(F32) + bg_ref[...])
    d = y_att.shape[1]
    mix = gates[:, :d] * y_att + gates[:, d:] * y_rwkv
    out_ref[...] = x_ref[...] + gt_ref[0] * _dot(mix.astype(BF16), wo_ref[...])


def _mix(att_o, att_l, rw, zall, b_gate, x2, gt1, wa, wr, wo, batch, seq, tm):
    m, d = x2.shape
    per_b = seq // tm
    row = lambda w: pl.BlockSpec((tm, w), lambda i: (i, 0))
    full = lambda a: pl.BlockSpec(a.shape, lambda i: (0, 0))

    def dil(arr):
        dd = arr.shape[1]
        return pl.BlockSpec((1, dd, tm // dd, arr.shape[3]), lambda i: (i // per_b, 0, i % per_b, 0))

    hexp = (jnp.arange(LANES)[:, None] == (jnp.arange(ATT_WIDTH) // HEAD_DIM)[None, :]).astype(BF16)
    return pl.pallas_call(
        _mix_kernel,
        grid=(m // tm,),
        in_specs=[dil(a) for a in att_o] + [dil(a) for a in att_l] +
                 [row(d), row(2 * d), full(b_gate), row(d),
                  pl.BlockSpec((1, 1, d), lambda i: (i // per_b, 0, 0)), full(hexp), full(wa), full(wr), full(wo)],
        out_specs=row(d),
        out_shape=jax.ShapeDtypeStruct((m, d), F32),
        scratch_shapes=[pltpu.VMEM((ATT_WIDTH // LANES, tm, LANES), F32), pltpu.VMEM((tm, LANES), F32)],
        compiler_params=_params(("parallel",)),
    )(*att_o, *att_l, rw, zall, b_gate, x2, gt1, hexp, wa, wr, wo)


HALO = 16


def _ffn_kernel(x_ref, xh_ref, nw_ref, sc_ref, sh_ref, gt_ref, wu_ref, cw_ref, cb_ref, wd_ref, nf_ref, o_ref,
                *, per_b, tf):
    i = pl.program_id(0)
    tm = x_ref.shape[0]
    f = wd_ref.shape[0]
    x = x_ref[...]
    hh = _adaln(xh_ref[...], nw_ref[...], sc_ref[0], sh_ref[0])
    h = jnp.concatenate([jnp.where(i % per_b == 0, 0.0, hh).astype(BF16),
                         _adaln(x, nw_ref[...], sc_ref[0], sh_ref[0]).astype(BF16)], axis=0)

    def conv(u, cols):
        out = cb_ref[:, cols] + cw_ref[2:3, cols] * u[HALO:, :]
        out = out + cw_ref[1:2, cols] * u[HALO - 1:HALO - 1 + tm, :]
        return out + cw_ref[0:1, cols] * u[HALO - 2:HALO - 2 + tm, :]

    acc = None
    for c0 in range(0, f, tf):
        gcols = slice(c0, c0 + tf)
        vcols = slice(f + c0, f + c0 + tf)
        gate = conv(_dot(h, wu_ref[:, gcols]), gcols)
        val = conv(_dot(h, wu_ref[:, vcols]), vcols)
        act = (gate * jax.nn.sigmoid(gate) * val).astype(BF16)
        part = _dot(act, wd_ref[gcols, :])
        acc = part if acc is None else acc + part
    x2 = x + gt_ref[0] * acc
    y = x2 * lax.rsqrt(jnp.mean(x2 * x2, axis=-1, keepdims=True) + RMS_EPS)
    o_ref[...] = y * nf_ref[...]


def _ffn(x1, nw, sc, sh, gt, w_up, conv_w, conv_b, w_down, nf, seq, tm, tf):
    m, d = x1.shape
    f = w_down.shape[0]
    per_b = seq // tm
    hb = tm // HALO
    rowc = lambda i: (i, 0)
    fixed = lambda i: (0, 0)
    bvec = pl.BlockSpec((1, 1, d), lambda i: (i // per_b, 0, 0))
    resident = lambda shape: pl.BlockSpec(shape, fixed, pipeline_mode=pl.Buffered(1))
    return pl.pallas_call(
        functools.partial(_ffn_kernel, per_b=per_b, tf=tf),
        grid=(m // tm,),
        in_specs=[pl.BlockSpec((tm, d), rowc),
                  pl.BlockSpec((HALO, d), lambda i: (jnp.maximum(i * hb - 1, 0), 0)),
                  pl.BlockSpec((1, d), fixed), bvec, bvec, bvec,
                  resident((d, 2 * f)), resident((CONV_WIDTH, 2 * f)), resident((1, 2 * f)), resident((f, d)),
                  pl.BlockSpec((1, d), fixed)],
        out_specs=pl.BlockSpec((tm, d), rowc),
        out_shape=jax.ShapeDtypeStruct((m, d), F32),
        compiler_params=pltpu.CompilerParams(dimension_semantics=("parallel",), vmem_limit_bytes=FFN_VMEM_LIMIT),
    )(x1, x1, nw, sc, sh, gt, w_up, conv_w, conv_b.reshape(1, 2 * f), w_down, nf)


def kernel(x, c, w_ada, b_ada, norm1_w, w_in, b_gate, mu_shift, w0, w2, a0, a2, g2, k_k, k_a, r_k, lnx_w, lnx_b,
           w_att_out, w_rwkv_out, w_o, norm2_w, w_up, conv_w, conv_b, w_down, norm_f_w):
    batch, seq, d = x.shape
    assert w_ada.shape[0] == 1, "the fused ffn kernel applies the final RMSNorm; one layer only"
    l = 0
    grp = 3 * ATT_WIDTH
    att_in = len(ATT_PATTERNS) * grp
    lora = DECAY_LORA + AAA_LORA + GATE_LORA
    xf = x.reshape(batch * seq, d)

    ada = _ada(c, w_ada[l], b_ada[l])
    sh1, sc1, gt1, sh2, sc2, gt2 = [t.reshape(batch, 1, d) for t in jnp.split(ada, 6, axis=-1)]
    h = _norm(xf, norm1_w[l].reshape(1, d), sc1, sh1, seq, NORM_TM)

    win = w_in[l].astype(BF16)
    w_rest = jnp.concatenate(
        [win[:, att_in + 3 * d + lora:], win[:, att_in:att_in + 3 * d + lora],
         jnp.zeros((d, LORA_PAD - lora), BF16)], axis=1)
    mu_all = jnp.concatenate([jnp.zeros((2 * d,), F32), mu_shift[l], jnp.zeros((LORA_PAD - lora,), F32)])
    zall = _mm_shift(h, w_rest, mu_all.reshape(1, -1), BF16, seq, ZALL_TM, ZALL_TN)

    att_o, att_l = [], []
    qkv_scale = jnp.concatenate([jnp.full((ATT_WIDTH,), HEAD_DIM ** -0.5, F32),
                                 jnp.ones((2 * ATT_WIDTH,), F32)]).reshape(1, grp)
    for gi, (_, dilation) in enumerate(ATT_PATTERNS):
        qkv = _mm_dilate(h, win, gi * grp, qkv_scale, dilation, batch, seq, QKV_TM, QKV_TN)
        o, lse = _attention(qkv.reshape(batch * seq, grp), seq // dilation // ATT_BLOCK)
        att_o.append(o.reshape(batch, dilation, seq // dilation, ATT_WIDTH))
        att_l.append(lse.reshape(batch, dilation, seq // dilation, LANES))

    vec = lambda t: t.reshape(1, d)
    rw = _rwkv(zall, 2 * d, vec(w0[l]), w2[l].astype(BF16), vec(a0[l]), a2[l].astype(BF16), g2[l].astype(BF16),
               vec(k_k[l]), vec(k_a[l]), vec(r_k[l]), vec(lnx_w[l]), vec(lnx_b[l]), batch, seq)

    x1 = _mix(att_o, att_l, rw, zall, b_gate[l].reshape(1, 2 * d), xf, gt1,
              w_att_out[l].astype(BF16), w_rwkv_out[l].astype(BF16), w_o[l].astype(BF16), batch, seq, MIX_TM)
    out = _ffn(x1, norm2_w[l].reshape(1, d), sc2, sh2, gt2, w_up[l].astype(BF16), conv_w[l], conv_b[l],
               w_down[l].astype(BF16), norm_f_w.reshape(1, d), seq, FFN_TM, FFN_TF)
    return out.reshape(batch, seq, d)
```

```python
import functools
import math

import jax
import jax.numpy as jnp
from jax import lax
from jax.experimental import pallas as pl
from jax.experimental.pallas import tpu as pltpu

F32 = jnp.float32
BF16 = jnp.bfloat16

LANES = 128
ATT_PATTERNS = ((128, 1), (512, 4), (2048, 16))
ATT_HEADS = 8
HEAD_DIM = 64
ATT_WIDTH = ATT_HEADS * HEAD_DIM
ATT_BLOCK = 128
ATT_TQ = 1024
RWKV_HEADS = 16
DECAY_LORA = 64
AAA_LORA = 64
GATE_LORA = 160
LORA_PAD = 512
SHIFT_SLABS = 4
ADA_TN = 1536
NORM_TM = 2048
ZALL_TM, ZALL_TN = 1024, 2816
QKV_TM, QKV_TN = 1024, 3 * ATT_WIDTH
MIX_TM = 512
FFN_TM, FFN_TF = 512, 2816
CONV_WIDTH = 3
RMS_EPS = 1e-6
GN_EPS = 64e-5
DECAY_SCALE = math.exp(-0.5)
CHUNK = 64
CHUNKS_PER_STEP = 8
VMEM_LIMIT = 48 * 1024 * 1024
FFN_VMEM_LIMIT = 56 * 1024 * 1024


def _dot(a, b):
    return jnp.dot(a, b, preferred_element_type=F32)


def _dot_nt(a, b):
    return lax.dot_general(a, b, (((1,), (1,)), ((), ())), preferred_element_type=F32)


def _split3(x):
    hi = x.astype(BF16)
    r1 = x - hi.astype(F32)
    mid = r1.astype(BF16)
    lo = (r1 - mid.astype(F32)).astype(BF16)
    return hi, mid, lo


def _dot2(x, w):
    hi = x.astype(BF16)
    lo = (x - hi.astype(F32)).astype(BF16)
    return _dot(hi, w) + _dot(lo, w)


def _trace_together(*stagewise):
    results = [None] * len(stagewise)
    live = list(enumerate(stagewise))
    while live:
        for item in list(live):
            try:
                next(item[1])
            except StopIteration as done:
                results[item[0]] = done.value
                live.remove(item)
    return results


def _params(sem):
    return pltpu.CompilerParams(dimension_semantics=sem, vmem_limit_bytes=VMEM_LIMIT)


def _ada_kernel(c_ref, w_ref, b_ref, o_ref):
    ch, cl, _ = _split3(c_ref[...])
    wh, wl, _ = _split3(w_ref[...])
    o_ref[...] = _dot(ch, wh) + _dot(cl, wh) + _dot(ch, wl) + b_ref[...]


def _ada(c, w_ada, b_ada):
    b, d = c.shape
    n = w_ada.shape[1]
    tn = ADA_TN
    return pl.pallas_call(
        _ada_kernel,
        grid=(n // tn,),
        in_specs=[pl.BlockSpec((b, d), lambda j: (0, 0)),
                  pl.BlockSpec((d, tn), lambda j: (0, j)),
                  pl.BlockSpec((1, tn), lambda j: (0, j))],
        out_specs=pl.BlockSpec((b, tn), lambda j: (0, j)),
        out_shape=jax.ShapeDtypeStruct((b, n), F32),
        compiler_params=_params(("arbitrary",)),
    )(c, w_ada, b_ada.reshape(1, n))


def _adaln(x, nw, sc, sh):
    y = x * lax.rsqrt(jnp.mean(x * x, axis=-1, keepdims=True) + RMS_EPS)
    return (y * nw) * (1.0 + sc) + sh


def _norm_kernel(x_ref, nw_ref, sc_ref, sh_ref, o_ref):
    o_ref[...] = _adaln(x_ref[...], nw_ref[...], sc_ref[0], sh_ref[0]).astype(o_ref.dtype)


def _norm(x2, nw, sc, sh, seq, tm):
    m, d = x2.shape
    per_b = seq // tm
    bvec = pl.BlockSpec((1, 1, d), lambda i: (i // per_b, 0, 0))
    return pl.pallas_call(
        _norm_kernel,
        grid=(m // tm,),
        in_specs=[pl.BlockSpec((tm, d), lambda i: (i, 0)), pl.BlockSpec((1, d), lambda i: (0, 0)), bvec, bvec],
        out_specs=pl.BlockSpec((tm, d), lambda i: (i, 0)),
        out_shape=jax.ShapeDtypeStruct((m, d), BF16),
        compiler_params=_params(("parallel",)),
    )(x2, nw, sc, sh)


def _mm_shift_kernel(a_ref, w_ref, mu_ref, o_ref, carry_ref, *, per_b):
    i = pl.program_id(0)
    j = pl.program_id(1)
    tm = a_ref.shape[0]

    @pl.when((i == 0) & (j == 0))
    def _():
        carry_ref[...] = jnp.zeros_like(carry_ref)

    before = jnp.where(i % per_b == 0, 0.0, carry_ref[j])
    w = w_ref[...]
    mu = mu_ref[...]
    slab = tm // SHIFT_SLABS
    first8 = lax.broadcasted_iota(jnp.int32, (8, w.shape[1]), 0) == 0
    outs = []
    for s in range(SHIFT_SLABS):
        z = _dot(a_ref[s * slab:(s + 1) * slab, :], w)
        rolled = pltpu.roll(z, 1, axis=0)
        prev = jnp.concatenate([jnp.where(first8, before, rolled[:8]), rolled[8:]], axis=0)
        outs.append((z + (prev - z) * mu).astype(o_ref.dtype))
        before = z[slab - 1:slab, :]
    carry_ref[j] = before
    o_ref[...] = jnp.concatenate(outs, axis=0)


def _mm_shift(a, w, mu, out_dtype, seq, tm, tn):
    m, k = a.shape
    n = w.shape[1]
    return pl.pallas_call(
        functools.partial(_mm_shift_kernel, per_b=seq // tm),
        grid=(m // tm, n // tn),
        in_specs=[pl.BlockSpec((tm, k), lambda i, j: (i, 0)), pl.BlockSpec((k, tn), lambda i, j: (0, j)),
                  pl.BlockSpec((1, tn), lambda i, j: (0, j))],
        out_specs=pl.BlockSpec((tm, tn), lambda i, j: (i, j)),
        out_shape=jax.ShapeDtypeStruct((m, n), out_dtype),
        scratch_shapes=[pltpu.VMEM((n // tn, 1, tn), F32)],
        compiler_params=_params(("arbitrary", "arbitrary")),
    )(a, w, mu)


MAX_ROW_STRIDE = 4


def _mm_dilate_kernel(a_ref, w_ref, cs_ref, o_ref, acc_ref, tmp_ref, *, dilation):
    tm = a_ref.shape[0]
    rows = tm // dilation
    acc = _dot(a_ref[...], w_ref[...]) * cs_ref[...]
    if dilation == 1:
        o_ref[0, 0] = acc.astype(o_ref.dtype)
        return
    nct = acc.shape[1] // LANES
    for c in range(nct):
        acc_ref[c] = acc[:, c * LANES:(c + 1) * LANES]

    def emit(r, src_ref, start, stride):
        o_ref[0, r] = jnp.concatenate(
            [src_ref[c, pl.ds(start, rows, stride=stride), :] for c in range(nct)], axis=-1).astype(o_ref.dtype)

    if dilation <= MAX_ROW_STRIDE:
        for r in range(dilation):
            emit(r, acc_ref, r, dilation)
        return
    s1 = MAX_ROW_STRIDE
    s2 = dilation // s1
    part = tm // s1
    for q in range(s1):
        for c in range(nct):
            tmp_ref[c, q * part:(q + 1) * part, :] = acc_ref[c, pl.ds(q, part, stride=s1), :]
    for q in range(s1):
        for q2 in range(s2):
            emit(q + s1 * q2, tmp_ref, q * part + q2, s2)


def _mm_dilate(a, w, col0, col_scale, dilation, batch, seq, tm, tn):
    m, k = a.shape
    n = col_scale.shape[1]
    cb = col0 // tn
    per_b = seq // tm
    rows = tm // dilation
    return pl.pallas_call(
        functools.partial(_mm_dilate_kernel, dilation=dilation),
        grid=(m // tm, n // tn),
        in_specs=[pl.BlockSpec((tm, k), lambda i, j: (i, 0)), pl.BlockSpec((k, tn), lambda i, j: (0, cb + j)),
                  pl.BlockSpec((1, tn), lambda i, j: (0, j))],
        out_specs=pl.BlockSpec((1, dilation, rows, tn), lambda i, j: (i // per_b, 0, i % per_b, j)),
        out_shape=jax.ShapeDtypeStruct((batch, dilation, seq // dilation, n), BF16),
        scratch_shapes=[pltpu.VMEM((tn // LANES, tm, LANES), F32)] * 2,
        compiler_params=_params(("parallel", "arbitrary")),
    )(a, w, col_scale)


def _att_kernel(q_ref, k_ref, kh_ref, v_ref, vh_ref, o_ref, l_ref, *, blocks_per_seq):
    i = pl.program_id(0)
    nq = q_ref.shape[0] // ATT_BLOCK
    qi = lax.broadcasted_iota(jnp.int32, (ATT_BLOCK, 2 * ATT_BLOCK), 0)
    kj = lax.broadcasted_iota(jnp.int32, (ATT_BLOCK, 2 * ATT_BLOCK), 1)
    band = (kj >= qi) & (kj <= qi + ATT_BLOCK)
    neg = jnp.float32(-1e30)
    bias_std = jnp.where(band, 0.0, neg)
    bias_first = jnp.where(band & (kj >= ATT_BLOCK), 0.0, neg)
    lane = lax.broadcasted_iota(jnp.int32, (ATT_BLOCK, LANES), 1)
    def block(qb):
        rows = slice(qb * ATT_BLOCK, (qb + 1) * ATT_BLOCK)
        prev = slice((qb - 1) * ATT_BLOCK, qb * ATT_BLOCK)
        first = (i * nq + qb) % blocks_per_seq == 0
        bias = jnp.where(first, bias_first, bias_std)
        q = q_ref[rows, :]
        k = jnp.concatenate([kh_ref[...] if qb == 0 else k_ref[prev, :], k_ref[rows, :]], axis=0)
        v = jnp.concatenate([vh_ref[...] if qb == 0 else v_ref[prev, :], v_ref[rows, :]], axis=0)
        heads = range(ATT_HEADS)
        tile = lambda h: slice((h // 2) * LANES, (h // 2 + 1) * LANES)
        own = lambda h: (lane < HEAD_DIM) if h % 2 == 0 else (lane >= HEAD_DIM)
        s = [_dot_nt(jnp.where(own(h), q[:, tile(h)], jnp.zeros((), q.dtype)), k[:, tile(h)]) + bias for h in heads]
        m = [jnp.max(s[h], axis=-1, keepdims=True) for h in heads]
        p = [jnp.exp(s[h] - m[h]) for h in heads]
        den = [jnp.sum(p[h], axis=-1, keepdims=True) for h in heads]
        pv = [_dot(p[h].astype(BF16), v[:, tile(h)]) / den[h] for h in heads]
        lse_tile = jnp.zeros((ATT_BLOCK, LANES), F32)
        for h in heads:
            lse_tile = jnp.where(lane == h, m[h] + jnp.log(den[h]), lse_tile)
        outs = [jnp.where(lane < HEAD_DIM, pv[2 * hp], pv[2 * hp + 1]) for hp in range(ATT_HEADS // 2)]
        return jnp.concatenate(outs, axis=-1).astype(o_ref.dtype), lse_tile

    done = [block(qb) for qb in range(nq)]
    o_ref[...] = jnp.concatenate([o for o, _ in done], axis=0)
    l_ref[...] = jnp.concatenate([l for _, l in done], axis=0)


def _attention(qkv, blocks_per_seq):
    rows = qkv.shape[0]
    tq = ATT_TQ
    hb = tq // ATT_BLOCK
    main = lambda c: pl.BlockSpec((tq, ATT_WIDTH), lambda i: (i, c))
    halo = lambda c: pl.BlockSpec((ATT_BLOCK, ATT_WIDTH), lambda i: (jnp.maximum(i * hb - 1, 0), c))
    return pl.pallas_call(
        functools.partial(_att_kernel, blocks_per_seq=blocks_per_seq),
        grid=(rows // tq,),
        in_specs=[main(0), main(1), halo(1), main(2), halo(2)],
        out_specs=[pl.BlockSpec((tq, ATT_WIDTH), lambda i: (i, 0)), pl.BlockSpec((tq, LANES), lambda i: (i, 0))],
        out_shape=[jax.ShapeDtypeStruct((rows, ATT_WIDTH), BF16), jax.ShapeDtypeStruct((rows, LANES), F32)],
        compiler_params=_params(("parallel",)),
    )(qkv, qkv, qkv, qkv, qkv)


def _rwkv_kernel(zr_ref, zk_ref, zv_ref, zl_ref, w0_ref, w2_ref, a0_ref, a2_ref, g2_ref, kk_ref, ka_ref, rk_ref,
                 tri_ref, hsum_ref, hexp_ref, lnw_ref, lnb_ref, o_ref, state_ref):
    C = CHUNK
    N = HEAD_DIM

    @pl.when(pl.program_id(1) == 0)
    def _():
        state_ref[...] = jnp.zeros_like(state_ref)

    r = zr_ref[...].astype(F32)
    k = zk_ref[...].astype(F32)
    v = zv_ref[...].astype(F32)
    zl = zl_ref[...].astype(F32)
    w_low = zl[:, :DECAY_LORA]
    a_low = zl[:, DECAY_LORA:DECAY_LORA + AAA_LORA]
    g_low = zl[:, DECAY_LORA + AAA_LORA:DECAY_LORA + AAA_LORA + GATE_LORA]

    wpre = w0_ref[...] + _dot(jnp.tanh(w_low).astype(BF16), w2_ref[...])
    lw = jax.nn.sigmoid(wpre) * (-DECAY_SCALE)
    a = jax.nn.sigmoid(a0_ref[...] + _dot(a_low.astype(BF16), a2_ref[...]))
    g = _dot(jax.nn.sigmoid(g_low).astype(BF16), g2_ref[...])
    km = k * (1.0 + (a - 1.0) * ka_ref[...])

    kn = k * kk_ref[...]
    inv = lax.rsqrt(jnp.maximum(_dot2(kn * kn, hsum_ref[...]), 1e-24))
    kn = kn * _dot2(inv, hexp_ref[...])
    rk_sum = _dot((r * km * rk_ref[...]).astype(BF16), hsum_ref[...])
    bonus = _dot(rk_sum.astype(BF16), hexp_ref[...]) * v
    vb = v.astype(BF16)

    P2 = 2 * N
    lane = lambda shape: lax.broadcasted_iota(jnp.int32, shape, 1)
    row = lambda shape: lax.broadcasted_iota(jnp.int32, shape, 0)
    lo_c = lane((C, P2)) < N
    bd = (row((P2, P2)) < N) == (lane((P2, P2)) < N)
    ri = row((2 * C, 4 * C))
    cj = lane((2 * C, 4 * C)) & (C - 1)
    keep = cj < (ri & (C - 1)) + jnp.where(ri >= C, 1, 0)
    eye2 = ((lane((C, P2)) & (N - 1)) == row((C, P2))).astype(F32)
    zero_b = jnp.zeros((), BF16)
    odd_head = (lane((C, r.shape[1])) & N) != 0
    l1 = lw.astype(BF16)
    l2 = (lw - l1.astype(F32)).astype(BF16)
    cum_all = _dot(tri_ref[...], l1) + _dot(tri_ref[...], l2)

    def blockdiag(m):
        return jnp.where(bd, jnp.concatenate([m, m], axis=0), jnp.zeros((), m.dtype))

    pairs = range(RWKV_HEADS // 2)
    each = lambda f: [f(p) for p in pairs]
    tl = lambda p: slice(p * P2, (p + 1) * P2)

    def chunk_dense(ci):
        rows = slice(ci * C, (ci + 1) * C)
        lwc = lw[rows]
        cum = cum_all[rows]
        e_pos = jnp.exp(cum)
        e_neg = jnp.exp(-cum)
        knc = kn[rows]
        xa = (-knc * jnp.exp(cum - lwc)).astype(BF16)
        xr = (r[rows] * e_pos).astype(BF16)
        zk = km[rows] * e_neg
        zb = knc * a[rows] * e_neg
        zt = jnp.concatenate([jnp.where(odd_head, zk, zb), jnp.where(odd_head, zb, zk)], axis=0).T.astype(BF16)
        return xa, xr, zt, e_pos[C - 1:C, :]

    def state_free(ci, dense):
        rows = slice(ci * C, (ci + 1) * C)
        xa, xr, zt, glast = dense
        X = each(lambda p: jnp.concatenate([xa[:, tl(p)], xr[:, tl(p)]], axis=0))
        ZT = each(lambda p: zt[tl(p), :])
        Vb = each(lambda p: vb[rows, tl(p)])
        A = each(lambda p: jnp.where(keep, _dot(X[p], jnp.concatenate(
            [jnp.where(bd, ZT[p], zero_b), jnp.where(bd, zero_b, ZT[p])], axis=1)), 0.0))
        yield
        Lb = each(lambda p: A[p][:C, :P2].astype(BF16))
        Pm = each(lambda p: _dot(Lb[p], blockdiag(Lb[p])))
        T = each(lambda p: eye2 + A[p][:C, :P2])
        yield
        span = 2
        while 2 * span < C:
            R = each(lambda p: _dot(jnp.concatenate([Pm[p], T[p]], axis=0).astype(BF16),
                                    blockdiag(Pm[p].astype(BF16))))
            Pm = each(lambda p: R[p][:C])
            T = each(lambda p: T[p] + R[p][C:])
            span *= 2
            yield
        Tb = each(lambda p: (T[p] + _dot(T[p].astype(BF16), blockdiag(Pm[p].astype(BF16)))).astype(BF16))
        yield
        XA = each(lambda p: jnp.concatenate([X[p], A[p][:, P2:].astype(BF16)], axis=1))
        Vad = each(lambda p: jnp.where(bd, zero_b, jnp.concatenate([Vb[p], Vb[p]], axis=0)))
        Arb = each(lambda p: A[p][C:, :P2].astype(BF16))
        gcol = each(lambda p: jnp.broadcast_to(glast[:, tl(p)], (P2, P2)).T)
        return XA, Vad, ZT, Vb, Tb, Arb, gcol

    def advance(pre, H_all):
        XA, Vad, ZT, Vb, Tb, Arb, gcol = pre
        XS = each(lambda p: _dot(XA[p], jnp.concatenate([H_all[p].astype(BF16), Vad[p]], axis=0)))
        yield
        W = each(lambda p: XS[p][:C].astype(BF16))
        U = each(lambda p: _dot(Tb[p], blockdiag(W[p])))
        yield
        Ub = each(lambda p: U[p].astype(BF16))
        Y = each(lambda p: XS[p][C:] + _dot(Arb[p], blockdiag(Ub[p])))
        yield
        UV = each(lambda p: jnp.concatenate([jnp.where(lo_c, Ub[p], Vb[p]), jnp.where(lo_c, Vb[p], Ub[p])], axis=0))
        H_new = each(lambda p: jnp.where(bd, H_all[p] + _dot(ZT[p], UV[p]), 0.0) * gcol[p])
        return Y, H_new

    def group_norm(y):
        def head_mean(t):
            s_lo = jnp.sum(jnp.where(lo_c, t, 0.0), axis=-1, keepdims=True)
            s_hi = jnp.sum(jnp.where(lo_c, 0.0, t), axis=-1, keepdims=True)
            return jnp.where(lo_c, s_lo, s_hi) * (1.0 / N)
        dlt = y - head_mean(y)
        return dlt * lax.rsqrt(head_mean(dlt * dlt) + GN_EPS)

    n_chunks = zr_ref.shape[0] // C
    H_all = each(lambda p: state_ref[p])
    ys = []
    (nxt,) = _trace_together(state_free(0, chunk_dense(0)))
    for ci in range(n_chunks):
        pre = nxt
        if ci + 1 < n_chunks:
            (Y, H_all), nxt = _trace_together(advance(pre, H_all), state_free(ci + 1, chunk_dense(ci + 1)))
        else:
            ((Y, H_all),) = _trace_together(advance(pre, H_all))
        ys.append(jnp.concatenate([group_norm(Y[p]) for p in pairs], axis=-1))
    state_ref[...] = jnp.stack([H_all[p] for p in pairs])
    y = jnp.concatenate(ys, axis=0) * lnw_ref[...] + lnb_ref[...]
    o_ref[...] = ((y + bonus) * g).astype(o_ref.dtype)


def _rwkv(zall, col0, w0, w2, a0, a2, g2, k_k, k_a, r_k, lnx_w, lnx_b, batch, seq):
    m = zall.shape[0]
    d = w0.shape[1]
    tm = CHUNK * CHUNKS_PER_STEP
    per_b = seq // tm
    cb = col0 // d
    row = lambda b, t: (b * per_b + t, 0)
    fixed = lambda b, t: (0, 0)
    zspec = lambda c: pl.BlockSpec((tm, d), lambda b, t: (b * per_b + t, cb + c))
    lspec = pl.BlockSpec((tm, LORA_PAD), lambda b, t: (b * per_b + t, (col0 + 3 * d) // LORA_PAD))
    vec = pl.BlockSpec((1, d), fixed)
    t_idx = jnp.arange(tm)
    tri = ((t_idx[:, None] >= t_idx[None, :]) &
           (t_idx[:, None] // CHUNK == t_idx[None, :] // CHUNK)).astype(BF16)
    head_of = jnp.arange(d) // HEAD_DIM
    hsum = (head_of[:, None] == jnp.arange(LANES)[None, :]).astype(BF16)
    hexp = hsum.T
    return pl.pallas_call(
        _rwkv_kernel,
        grid=(batch, per_b),
        in_specs=[zspec(0), zspec(1), zspec(2), lspec,
                  vec, pl.BlockSpec((DECAY_LORA, d), fixed),
                  vec, pl.BlockSpec((AAA_LORA, d), fixed),
                  pl.BlockSpec((GATE_LORA, d), fixed), vec, vec, vec,
                  pl.BlockSpec((tm, tm), fixed), pl.BlockSpec((d, LANES), fixed),
                  pl.BlockSpec((LANES, d), fixed), vec, vec],
        out_specs=pl.BlockSpec((tm, d), row),
        out_shape=jax.ShapeDtypeStruct((m, d), BF16),
        scratch_shapes=[pltpu.VMEM((RWKV_HEADS // 2, 2 * HEAD_DIM, 2 * HEAD_DIM), F32)],
        compiler_params=_params(("parallel", "arbitrary")),
    )(zall, zall, zall, zall, w0, w2, a0, a2, g2, k_k, k_a, r_k, tri, hsum, hexp, lnx_w, lnx_b)


def _mix_kernel(o1, o2, o3, l1, l2, l3, rw_ref, gate_ref, bg_ref, x_ref, gt_ref, hexp_ref, wa_ref, wr_ref, wo_ref,
                out_ref, os_ref, ls_ref):
    tm = x_ref.shape[0]

    def token_order(o_ref, l_ref):
        d = o_ref.shape[1]
        if d == 1:
            return o_ref[0, 0].astype(F32), l_ref[0, 0]
        rows = tm // d
        nct = os_ref.shape[0]
        for r in range(d):
            o = o_ref[0, r].astype(F32)
            for c in range(nct):
                os_ref[c, pl.ds(r, rows, stride=d), :] = o[:, c * LANES:(c + 1) * LANES]
            ls_ref[pl.ds(r, rows, stride=d), :] = l_ref[0, r]
        return jnp.concatenate([os_ref[c] for c in range(nct)], axis=-1), ls_ref[...]

    oa, la = token_order(o1, l1)
    ob, lb = token_order(o2, l2)
    oc, lc = token_order(o3, l3)
    mx = jnp.maximum(jnp.maximum(la, lb), lc)
    ea, eb, ec = jnp.exp(la - mx), jnp.exp(lb - mx), jnp.exp(lc - mx)
    inv = 1.0 / (ea + eb + ec)
    hexp = hexp_ref[...]
    att = (_dot2(ea * inv, hexp) * oa + _dot2(eb * inv, hexp) * ob + _dot2(ec * inv, hexp) * oc)
    y_att = _dot(att.astype(BF16), wa_ref[...])
    y_rwkv = _dot(rw_ref[...], wr_ref[...])
    gates = jax.nn.sigmoid(gate_ref[...].astype(F32) + bg_ref[...])
    d = y_att.shape[1]
    mix = gates[:, :d] * y_att + gates[:, d:] * y_rwkv
    out_ref[...] = x_ref[...] + gt_ref[0] * _dot(mix.astype(BF16), wo_ref[...])


def _mix(att_o, att_l, rw, zall, b_gate, x2, gt1, wa, wr, wo, batch, seq, tm):
    m, d = x2.shape
    per_b = seq // tm
    row = lambda w: pl.BlockSpec((tm, w), lambda i: (i, 0))
    full = lambda a: pl.BlockSpec(a.shape, lambda i: (0, 0))

    def dil(arr):
        dd = arr.shape[1]
        return pl.BlockSpec((1, dd, tm // dd, arr.shape[3]), lambda i: (i // per_b, 0, i % per_b, 0))

    hexp = (jnp.arange(LANES)[:, None] == (jnp.arange(ATT_WIDTH) // HEAD_DIM)[None, :]).astype(BF16)
    return pl.pallas_call(
        _mix_kernel,
        grid=(m // tm,),
        in_specs=[dil(a) for a in att_o] + [dil(a) for a in att_l] +
                 [row(d), row(2 * d), full(b_gate), row(d),
                  pl.BlockSpec((1, 1, d), lambda i: (i // per_b, 0, 0)), full(hexp), full(wa), full(wr), full(wo)],
        out_specs=row(d),
        out_shape=jax.ShapeDtypeStruct((m, d), F32),
        scratch_shapes=[pltpu.VMEM((ATT_WIDTH // LANES, tm, LANES), F32), pltpu.VMEM((tm, LANES), F32)],
        compiler_params=_params(("parallel",)),
    )(*att_o, *att_l, rw, zall, b_gate, x2, gt1, hexp, wa, wr, wo)


HALO = 16


def _ffn_kernel(x_ref, xh_ref, nw_ref, sc_ref, sh_ref, gt_ref, wu_ref, cw_ref, cb_ref, wd_ref, nf_ref, o_ref,
                *, per_b, tf):
    i = pl.program_id(0)
    tm = x_ref.shape[0]
    f = wd_ref.shape[0]
    x = x_ref[...]
    hh = _adaln(xh_ref[...], nw_ref[...], sc_ref[0], sh_ref[0])
    h = jnp.concatenate([jnp.where(i % per_b == 0, 0.0, hh).astype(BF16),
                         _adaln(x, nw_ref[...], sc_ref[0], sh_ref[0]).astype(BF16)], axis=0)

    def conv(u, cols):
        out = cb_ref[:, cols] + cw_ref[2:3, cols] * u[HALO:, :]
        out = out + cw_ref[1:2, cols] * u[HALO - 1:HALO - 1 + tm, :]
        return out + cw_ref[0:1, cols] * u[HALO - 2:HALO - 2 + tm, :]

    acc = None
    for c0 in range(0, f, tf):
        gcols = slice(c0, c0 + tf)
        vcols = slice(f + c0, f + c0 + tf)
        gate = conv(_dot(h, wu_ref[:, gcols]), gcols)
        val = conv(_dot(h, wu_ref[:, vcols]), vcols)
        act = (gate * jax.nn.sigmoid(gate) * val).astype(BF16)
        part = _dot(act, wd_ref[gcols, :])
        acc = part if acc is None else acc + part
    x2 = x + gt_ref[0] * acc
    y = x2 * lax.rsqrt(jnp.mean(x2 * x2, axis=-1, keepdims=True) + RMS_EPS)
    o_ref[...] = y * nf_ref[...]


def _ffn(x1, nw, sc, sh, gt, w_up, conv_w, conv_b, w_down, nf, seq, tm, tf):
    m, d = x1.shape
    f = w_down.shape[0]
    per_b = seq // tm
    hb = tm // HALO
    rowc = lambda i: (i, 0)
    fixed = lambda i: (0, 0)
    bvec = pl.BlockSpec((1, 1, d), lambda i: (i // per_b, 0, 0))
    resident = lambda shape: pl.BlockSpec(shape, fixed, pipeline_mode=pl.Buffered(1))
    return pl.pallas_call(
        functools.partial(_ffn_kernel, per_b=per_b, tf=tf),
        grid=(m // tm,),
        in_specs=[pl.BlockSpec((tm, d), rowc),
                  pl.BlockSpec((HALO, d), lambda i: (jnp.maximum(i * hb - 1, 0), 0)),
                  pl.BlockSpec((1, d), fixed), bvec, bvec, bvec,
                  resident((d, 2 * f)), resident((CONV_WIDTH, 2 * f)), resident((1, 2 * f)), resident((f, d)),
                  pl.BlockSpec((1, d), fixed)],
        out_specs=pl.BlockSpec((tm, d), rowc),
        out_shape=jax.ShapeDtypeStruct((m, d), F32),
        compiler_params=pltpu.CompilerParams(dimension_semantics=("parallel",), vmem_limit_bytes=FFN_VMEM_LIMIT),
    )(x1, x1, nw, sc, sh, gt, w_up, conv_w, conv_b.reshape(1, 2 * f), w_down, nf)


def kernel(x, c, w_ada, b_ada, norm1_w, w_in, b_gate, mu_shift, w0, w2, a0, a2, g2, k_k, k_a, r_k, lnx_w, lnx_b,
           w_att_out, w_rwkv_out, w_o, norm2_w, w_up, conv_w, conv_b, w_down, norm_f_w):
    batch, seq, d = x.shape
    assert w_ada.shape[0] == 1, "the fused ffn kernel applies the final RMSNorm; one layer only"
    l = 0
    grp = 3 * ATT_WIDTH
    att_in = len(ATT_PATTERNS) * grp
    lora = DECAY_LORA + AAA_LORA + GATE_LORA
    xf = x.reshape(batch * seq, d)

    ada = _ada(c, w_ada[l], b_ada[l])
    sh1, sc1, gt1, sh2, sc2, gt2 = [t.reshape(batch, 1, d) for t in jnp.split(ada, 6, axis=-1)]
    h = _norm(xf, norm1_w[l].reshape(1, d), sc1, sh1, seq, NORM_TM)

    win = w_in[l].astype(BF16)
    w_rest = jnp.concatenate(
        [win[:, att_in + 3 * d + lora:], win[:, att_in:att_in + 3 * d + lora],
         jnp.zeros((d, LORA_PAD - lora), BF16)], axis=1)
    mu_all = jnp.concatenate([jnp.zeros((2 * d,), F32), mu_shift[l], jnp.zeros((LORA_PAD - lora,), F32)])
    zall = _mm_shift(h, w_rest, mu_all.reshape(1, -1), BF16, seq, ZALL_TM, ZALL_TN)

    att_o, att_l = [], []
    qkv_scale = jnp.concatenate([jnp.full((ATT_WIDTH,), HEAD_DIM ** -0.5, F32),
                                 jnp.ones((2 * ATT_WIDTH,), F32)]).reshape(1, grp)
    for gi, (_, dilation) in enumerate(ATT_PATTERNS):
        qkv = _mm_dilate(h, win, gi * grp, qkv_scale, dilation, batch, seq, QKV_TM, QKV_TN)
        o, lse = _attention(qkv.reshape(batch * seq, grp), seq // dilation // ATT_BLOCK)
        att_o.append(o.reshape(batch, dilation, seq // dilation, ATT_WIDTH))
        att_l.append(lse.reshape(batch, dilation, seq // dilation, LANES))

    vec = lambda t: t.reshape(1, d)
    rw = _rwkv(zall, 2 * d, vec(w0[l]), w2[l].astype(BF16), vec(a0[l]), a2[l].astype(BF16), g2[l].astype(BF16),
               vec(k_k[l]), vec(k_a[l]), vec(r_k[l]), vec(lnx_w[l]), vec(lnx_b[l]), batch, seq)

    x1 = _mix(att_o, att_l, rw, zall, b_gate[l].reshape(1, 2 * d), xf, gt1,
              w_att_out[l].astype(BF16), w_rwkv_out[l].astype(BF16), w_o[l].astype(BF16), batch, seq, MIX_TM)
    out = _ffn(x1, norm2_w[l].reshape(1, d), sc2, sh2, gt2, w_up[l].astype(BF16), conv_w[l], conv_b[l],
               w_down[l].astype(BF16), norm_f_w.reshape(1, d), seq, FFN_TM, FFN_TF)
    return out.reshape(batch, seq, d)
```
